```python
import math
import jax, jax.numpy as jnp
from jax import lax
import numpy as np

D_MODEL = 4096
BATCH = 8
SEQ = 2048
DEPTH = 2
DEC_BATCH = 16
DEC_SEQ = 32
PAST_LEN = 1024

CHUNK = 64
Q_BLOCK = 128
N_HEADS = 16
HEAD_DK = 64
HEAD_DV = 2 * HEAD_DK
ATTN_DIM = N_HEADS * HEAD_DV
QK_DIM = N_HEADS * 2 * HEAD_DK
CONV_DIM = D_MODEL - ATTN_DIM
CONV_W = 3
IN_DIM = 2 * QK_DIM + ATTN_DIM + 3 * CONV_DIM
D_FF = 7 * D_MODEL // 2
N_EXPERTS = 8
TOP_K = 2
D_FF_EXPERT = 7 * D_MODEL // 2
N_DENSE = (DEPTH + 1) // 2
N_MOE = DEPTH // 2
EPS = 1e-6
NEG_INF = -1e30

kernel_name = "hybrid_diffattn_shortconv_stream_step"


def rmsnorm(x, g):
    xf = x.astype(jnp.float32)
    y = xf * lax.rsqrt(jnp.mean(xf * xf, axis=-1, keepdims=True) + EPS)
    return (y * g.astype(jnp.float32)).astype(x.dtype)


def lambda_init(l):
    return 0.8 - 0.6 * math.exp(-0.3 * l)


def diff_lambda(l, lq1, lk1, lq2, lk2):
    f = jnp.float32
    return (jnp.exp(jnp.sum(lq1[l].astype(f) * lk1[l].astype(f)))
            - jnp.exp(jnp.sum(lq2[l].astype(f) * lk2[l].astype(f))) + lambda_init(l))


def diff_weights(s, lam):
    p = jax.nn.softmax(s, axis=-1)
    return p[:, :, 0] - lam * p[:, :, 1]


def prompt_attention(q, k, v, lam):
    B, S = q.shape[0], q.shape[1]
    nb = S // Q_BLOCK
    scale = HEAD_DK ** -0.5
    qb = jnp.swapaxes(q.reshape(B, nb, Q_BLOCK, N_HEADS, 2, HEAD_DK), 0, 1)
    kf = k.astype(jnp.float32)
    vf = v.astype(jnp.float32)
    key_chunk = jnp.arange(S) // CHUNK

    def block(args):
        qi, i = args
        s = jnp.einsum('bqhmd,bkhmd->bhmqk', qi.astype(jnp.float32), kf) * scale
        q_chunk = (i * Q_BLOCK + jnp.arange(Q_BLOCK)) // CHUNK
        mask = key_chunk[None, :] <= q_chunk[:, None]
        s = jnp.where(mask, s, NEG_INF)
        w = diff_weights(s, lam)
        return jnp.einsum('bhqk,bkhv->bqhv', w, vf)

    o = lax.map(block, (qb, jnp.arange(nb)))
    return jnp.swapaxes(o, 0, 1).reshape(B, S, N_HEADS, HEAD_DV)


def sample_attention(q, k, v, lam, past_k, past_v):
    scale = HEAD_DK ** -0.5
    k_all = jnp.concatenate([past_k, k], axis=1).astype(jnp.float32)
    v_all = jnp.concatenate([past_v, v], axis=1).astype(jnp.float32)
    s = jnp.einsum('bqhmd,bkhmd->bhmqk', q.astype(jnp.float32), k_all) * scale
    w = diff_weights(s, lam)
    return jnp.einsum('bhqk,bkhv->bqhv', w, v_all)


def mixer_layer(x, l, attend, conv_state, w_in, w_out, conv_w, attn_norm, q_norm, k_norm,
                lambda_q1, lambda_k1, lambda_q2, lambda_k2, subln):
    B, S = x.shape[0], x.shape[1]
    h = rmsnorm(x, attn_norm[l])
    p = h @ w_in[l]
    q, k, v, bg, cg, hc = jnp.split(
        p, [QK_DIM, 2 * QK_DIM, 2 * QK_DIM + ATTN_DIM,
            2 * QK_DIM + ATTN_DIM + CONV_DIM, 2 * QK_DIM + ATTN_DIM + 2 * CONV_DIM], axis=-1)
    q = rmsnorm(q.reshape(B, S, N_HEADS, 2, HEAD_DK), q_norm[l])
    k = rmsnorm(k.reshape(B, S, N_HEADS, 2, HEAD_DK), k_norm[l])
    v = v.reshape(B, S, N_HEADS, HEAD_DV)
    lam = diff_lambda(l, lambda_q1, lambda_k1, lambda_q2, lambda_k2)
    o = attend(q, k, v, lam)
    o = (rmsnorm(o, subln[l]) * (1.0 - lambda_init(l))).astype(x.dtype)
    u = cg * hc
    u_pad = jnp.concatenate([conv_state.astype(u.dtype), u], axis=1)
    cw = conv_w[l]
    yc = cw[0] * u_pad[:, 0:S] + cw[1] * u_pad[:, 1:S + 1] + cw[2] * u_pad[:, 2:S + 2]
    yc = bg * yc
    new_conv = u_pad[:, -(CONV_W - 1):]
    mix = jnp.concatenate([o.reshape(B, S, ATTN_DIM), yc], axis=-1) @ w_out[l]
    return x + mix, k, v, new_conv


def swiglu(h, w1, w3, w2):
    return (jax.nn.silu(h @ w1) * (h @ w3)) @ w2


def moe_ffn(h, w_router, w1, w3, w2):
    shp = h.shape
    t = h.reshape(-1, shp[-1])
    logits = (t @ w_router).astype(jnp.float32)
    top_val, top_idx = lax.top_k(logits, TOP_K)
    top_w = jax.nn.softmax(top_val, axis=-1)
    gate = jnp.sum(jax.nn.one_hot(top_idx, N_EXPERTS, dtype=jnp.float32) * top_w[..., None], axis=-2)
    y = jnp.zeros(t.shape, jnp.float32)
    for e in range(N_EXPERTS):
        y = y + gate[:, e:e + 1] * swiglu(t, w1[e], w3[e], w2[e]).astype(jnp.float32)
    return y.astype(h.dtype).reshape(shp)


def ffn_layer(x, l, ffn_norm, w1_dense, w3_dense, w2_dense, w_router, w1_exp, w3_exp, w2_exp):
    h = rmsnorm(x, ffn_norm[l])
    m = l // 2
    if l % 2 == 0:
        return x + swiglu(h, w1_dense[m], w3_dense[m], w2_dense[m])
    return x + moe_ffn(h, w_router[m], w1_exp[m], w3_exp[m], w2_exp[m])


def setup_inputs(seed: int = 0) -> dict:
    key = jax.random.key(seed)
    ks = jax.random.split(key, 24)
    f = jnp.float32
    n = lambda k, shp, s: jax.random.normal(k, shp, f) * s
    return {
        "x_prompt": n(ks[0], (BATCH, SEQ, D_MODEL), 1.0),
        "x_sample": n(ks[1], (DEC_BATCH, DEC_SEQ, D_MODEL), 1.0),
        "cache_k": n(ks[2], (DEPTH, DEC_BATCH, PAST_LEN, N_HEADS, 2, HEAD_DK), 1.0),
        "cache_v": n(ks[3], (DEPTH, DEC_BATCH, PAST_LEN, N_HEADS, HEAD_DV), 1.0),
        "cache_conv": n(ks[4], (DEPTH, DEC_BATCH, CONV_W - 1, CONV_DIM), 1.0),
        "w_in": n(ks[5], (DEPTH, D_MODEL, IN_DIM), D_MODEL ** -0.5),
        "w_out": n(ks[6], (DEPTH, ATTN_DIM + CONV_DIM, D_MODEL), (ATTN_DIM + CONV_DIM) ** -0.5),
        "conv_w": n(ks[7], (DEPTH, CONV_W, CONV_DIM), CONV_W ** -0.5),
        "attn_norm": 1.0 + n(ks[8], (DEPTH, D_MODEL), 0.02),
        "q_norm": 1.0 + n(ks[9], (DEPTH, HEAD_DK), 0.02),
        "k_norm": 1.0 + n(ks[10], (DEPTH, HEAD_DK), 0.02),
        "lambda_q1": n(ks[11], (DEPTH, HEAD_DK), 0.1),
        "lambda_k1": n(ks[12], (DEPTH, HEAD_DK), 0.1),
        "lambda_q2": n(ks[13], (DEPTH, HEAD_DK), 0.1),
        "lambda_k2": n(ks[14], (DEPTH, HEAD_DK), 0.1),
        "subln": 1.0 + n(ks[15], (DEPTH, HEAD_DV), 0.02),
        "ffn_norm": 1.0 + n(ks[16], (DEPTH, D_MODEL), 0.02),
        "w1_dense": n(ks[17], (N_DENSE, D_MODEL, D_FF), D_MODEL ** -0.5),
        "w3_dense": n(ks[18], (N_DENSE, D_MODEL, D_FF), D_MODEL ** -0.5),
        "w2_dense": n(ks[19], (N_DENSE, D_FF, D_MODEL), D_FF ** -0.5),
        "w_router": n(ks[20], (N_MOE, D_MODEL, N_EXPERTS), D_MODEL ** -0.5),
        "w1_exp": n(ks[21], (N_MOE, N_EXPERTS, D_MODEL, D_FF_EXPERT), D_MODEL ** -0.5),
        "w3_exp": n(ks[22], (N_MOE, N_EXPERTS, D_MODEL, D_FF_EXPERT), D_MODEL ** -0.5),
        "w2_exp": n(ks[23], (N_MOE, N_EXPERTS, D_FF_EXPERT, D_MODEL), D_FF_EXPERT ** -0.5),
    }


def reference(x_prompt, x_sample, cache_k, cache_v, cache_conv, w_in, w_out, conv_w, attn_norm,
              q_norm, k_norm, lambda_q1, lambda_k1, lambda_q2, lambda_k2, subln, ffn_norm,
              w1_dense, w3_dense, w2_dense, w_router, w1_exp, w3_exp, w2_exp):
    mix_w = (w_in, w_out, conv_w, attn_norm, q_norm, k_norm,
             lambda_q1, lambda_k1, lambda_q2, lambda_k2, subln)
    ffn_w = (ffn_norm, w1_dense, w3_dense, w2_dense, w_router, w1_exp, w3_exp, w2_exp)
    xp, xs = x_prompt, x_sample
    kp, vp, cp, ksl, vsl, csl = [], [], [], [], [], []
    for l in range(DEPTH):
        zero_state = jnp.zeros((xp.shape[0], CONV_W - 1, CONV_DIM), xp.dtype)
        xp, k_new, v_new, conv_new = mixer_layer(xp, l, prompt_attention, zero_state, *mix_w)
        kp.append(k_new); vp.append(v_new); cp.append(conv_new)
        xp = ffn_layer(xp, l, *ffn_w)
        attend_s = lambda q, k, v, lam, _l=l: sample_attention(q, k, v, lam, cache_k[_l], cache_v[_l])
        xs, k_new, v_new, conv_new = mixer_layer(xs, l, attend_s, cache_conv[l], *mix_w)
        ksl.append(k_new); vsl.append(v_new); csl.append(conv_new)
        xs = ffn_layer(xs, l, *ffn_w)
    return (xp, xs, jnp.stack(kp), jnp.stack(vp), jnp.stack(cp),
            jnp.stack(ksl), jnp.stack(vsl), jnp.stack(csl))
```

```python
import functools
import math

import jax
import jax.numpy as jnp
from jax import lax
from jax.experimental import pallas as pl
from jax.experimental.pallas import tpu as pltpu

CHUNK = 64
TOP_K = 2
EPS = 1e-6
NEG_INF = -1e30
F32 = jnp.float32
BF16 = jnp.bfloat16

LANES = 128
MXU_DIM = 256
VMEM_LIMIT = 56 * 1024 * 1024

ROW_TILE = 1100
EXPERT_ROW_TILE = 1024
GATHER_ROWS = 256


def _lambda_init(l):
    return 0.8 - 0.6 * math.exp(-0.3 * l)


def _tile(n, pref, mult):
    if n <= pref:
        return n
    best = None
    for t in range(mult, pref + 1, mult):
        if n % t == 0:
            best = t
    assert best is not None, (n, pref, mult)
    return best


def _params(*sem):
    return pltpu.CompilerParams(dimension_semantics=sem, vmem_limit_bytes=VMEM_LIMIT)


def _dot(a, b):
    return jnp.dot(a, b, preferred_element_type=F32)


def _rmsnorm_kernel(x_ref, g_ref, o_ref):
    x = x_ref[...]
    y = x * lax.rsqrt(jnp.mean(x * x, axis=-1, keepdims=True) + EPS)
    o_ref[...] = (y * g_ref[...]).astype(o_ref.dtype)


def _rmsnorm(x, g):
    t, d = x.shape
    tr = _tile(t, 256, 16)
    return pl.pallas_call(
        _rmsnorm_kernel,
        out_shape=jax.ShapeDtypeStruct((t, d), BF16),
        grid=(t // tr,),
        in_specs=[pl.BlockSpec((tr, d), lambda i: (i, 0)),
                  pl.BlockSpec((1, d), lambda i: (0, 0))],
        out_specs=pl.BlockSpec((tr, d), lambda i: (i, 0)),
        compiler_params=_params("parallel"),
        name="rmsnorm",
    )(x, g.reshape(1, d))


def _mm_kernel(a_ref, b_ref, o_ref):
    o_ref[...] = _dot(a_ref[...], b_ref[...]).astype(o_ref.dtype)


def _matmul(a, b, out_dtype, tn_pref=1024):
    m, k = a.shape
    n = b.shape[1]
    tm = _tile(m, ROW_TILE, 16)
    tn = _tile(n, tn_pref, LANES)
    return pl.pallas_call(
        _mm_kernel,
        out_shape=jax.ShapeDtypeStruct((m, n), out_dtype),
        grid=(m // tm, n // tn),
        in_specs=[pl.BlockSpec((tm, k), lambda i, j: (i, 0)),
                  pl.BlockSpec((k, tn), lambda i, j: (0, j))],
        out_specs=pl.BlockSpec((tm, tn), lambda i, j: (i, j)),
        compiler_params=_params("parallel", "parallel"),
        name="in_proj",
    )(a, b)


def _groupnorm_kernel(p_ref, g_ref, gm_ref, o_ref, *, group, scale):
    x = p_ref[...]
    sq = x * x
    hi = sq.astype(BF16)
    lo = (sq - hi.astype(F32)).astype(BF16)
    gm = gm_ref[...]
    ssq = _dot(hi, gm) + _dot(lo, gm)
    y = x * lax.rsqrt(ssq * (1.0 / group) + EPS)
    o_ref[...] = ((y * g_ref[...]) * scale).astype(o_ref.dtype)


def _groupnorm(p, col_off, width, gain, group, scale, out_dtype):
    t = p.shape[0]
    tw = MXU_DIM
    assert width % tw == 0 and col_off % tw == 0 and tw % group == 0
    tr = _tile(t, ROW_TILE, 16)
    ids = jnp.arange(tw) // group
    gm = (ids[:, None] == ids[None, :]).astype(BF16)
    g = jnp.tile(gain.astype(F32), tw // group).reshape(1, tw)
    off = col_off // tw
    return pl.pallas_call(
        functools.partial(_groupnorm_kernel, group=group, scale=scale),
        out_shape=jax.ShapeDtypeStruct((t, width), out_dtype),
        grid=(t // tr, width // tw),
        in_specs=[pl.BlockSpec((tr, tw), lambda i, j: (i, j + off)),
                  pl.BlockSpec((1, tw), lambda i, j: (0, 0)),
                  pl.BlockSpec((tw, tw), lambda i, j: (0, 0))],
        out_specs=pl.BlockSpec((tr, tw), lambda i, j: (i, j)),
        compiler_params=_params("parallel", "parallel"),
        name="qk_norm",
    )(p, g, gm)


def _diff_lambda(lq1, lk1, lq2, lk2, lam_init):
    a = jnp.exp(jnp.sum(lq1[...] * lk1[...], axis=-1, keepdims=True))
    b = jnp.exp(jnp.sum(lq2[...] * lk2[...], axis=-1, keepdims=True))
    return a - b + lam_init


def _finish_heads(a1, l1, a2, l2, lam, sub, lam_init):
    o = a1 / l1 - lam * (a2 / l2)
    y = o * lax.rsqrt(jnp.mean(o * o, axis=-1, keepdims=True) + EPS)
    return (y * sub) * (1.0 - lam_init)


def _split_maps(q, dk):
    lane = lax.broadcasted_iota(jnp.int32, q.shape, 1)
    zero = jnp.zeros_like(q)
    return jnp.where(lane < dk, q, zero), jnp.where(lane >= dk, q, zero)


def _qk(q, k):
    return lax.dot_general(q, k, (((1,), (1,)), ((), ())), preferred_element_type=F32)


def _attn_prompt_kernel(q_ref, k_ref, v_ref, lq1, lk1, lq2, lk2, sub_ref, o_ref, kb, vb,
                        *, tq, dk, lam_init):
    qi = pl.program_id(2)

    @pl.when(qi == 0)
    def _():
        kb[...] = k_ref[...].astype(BF16)
        vb[...] = v_ref[...].astype(BF16)

    q1, q2 = _split_maps(q_ref[...], dk)
    dv = v_ref.shape[-1]

    def update(state, s, vs):
        m, l, a = state
        m_new = jnp.maximum(m, jnp.max(s, axis=-1, keepdims=True))
        alpha = jnp.exp(m - m_new)
        p = jnp.exp(s - m_new)
        l = alpha * l + jnp.sum(p, axis=-1, keepdims=True)
        a = alpha * a + _dot(p.astype(BF16), vs)
        return m_new, l, a

    def block(j, carry, masked):
        st1, st2 = carry
        start = pl.multiple_of(j * tq, tq)
        ks = kb[pl.ds(start, tq), :]
        vs = vb[pl.ds(start, tq), :]
        s1 = _qk(q1, ks)
        s2 = _qk(q2, ks)
        if masked:
            rc = lax.broadcasted_iota(jnp.int32, (tq, tq), 0) // CHUNK
            cc = lax.broadcasted_iota(jnp.int32, (tq, tq), 1) // CHUNK
            vis = cc <= rc
            s1 = jnp.where(vis, s1, NEG_INF)
            s2 = jnp.where(vis, s2, NEG_INF)
        return update(st1, s1, vs), update(st2, s2, vs)

    init = (jnp.full((tq, 1), NEG_INF, F32), jnp.zeros((tq, 1), F32), jnp.zeros((tq, dv), F32))
    carry = lax.fori_loop(0, qi, functools.partial(block, masked=False), (init, init))
    (_, l1, a1), (_, l2, a2) = block(qi, carry, True)
    lam = _diff_lambda(lq1, lk1, lq2, lk2, lam_init)
    o_ref[...] = _finish_heads(a1, l1, a2, l2, lam, sub_ref[...], lam_init).astype(o_ref.dtype)


def _attn_prompt(q, kv_src, k_col, v_src, v_col, nb, seq, heads, dk, lam_rows, sub, lam_init):
    hw = 2 * dk
    tq = _tile(seq, 256, CHUNK)
    nq = seq // tq
    vec = lambda i, h, j: (0, 0)
    return pl.pallas_call(
        functools.partial(_attn_prompt_kernel, tq=tq, dk=dk, lam_init=lam_init),
        out_shape=jax.ShapeDtypeStruct((nb * seq, heads * hw), BF16),
        grid=(nb, heads, nq),
        in_specs=[pl.BlockSpec((tq, hw), lambda b, h, j: (b * nq + j, h)),
                  pl.BlockSpec((seq, hw), lambda b, h, j: (b, k_col + h)),
                  pl.BlockSpec((seq, hw), lambda b, h, j: (b, v_col + h)),
                  pl.BlockSpec((1, dk), vec), pl.BlockSpec((1, dk), vec),
                  pl.BlockSpec((1, dk), vec), pl.BlockSpec((1, dk), vec),
                  pl.BlockSpec((1, hw), vec)],
        out_specs=pl.BlockSpec((tq, hw), lambda b, h, j: (b * nq + j, h)),
        scratch_shapes=[pltpu.VMEM((seq, hw), BF16), pltpu.VMEM((seq, hw), BF16)],
        compiler_params=_params("parallel", "parallel", "arbitrary"),
        name="attn_prompt",
    )(q, kv_src, v_src, *lam_rows, sub)


def _attn_sample_kernel(q_ref, kn_ref, vn_ref, kp_ref, vp_ref, lq1, lk1, lq2, lk2, sub_ref, o_ref,
                        *, dk, lam_init):
    q1, q2 = _split_maps(q_ref[...], dk)
    kp = kp_ref[...].astype(BF16)
    vp = vp_ref[...].astype(BF16)
    kn = kn_ref[...].astype(BF16)
    vn = vn_ref[...].astype(BF16)

    def one_map(q):
        sp = _qk(q, kp)
        sn = _qk(q, kn)
        m = jnp.maximum(jnp.max(sp, axis=-1, keepdims=True), jnp.max(sn, axis=-1, keepdims=True))
        ep = jnp.exp(sp - m)
        en = jnp.exp(sn - m)
        l = jnp.sum(ep, axis=-1, keepdims=True) + jnp.sum(en, axis=-1, keepdims=True)
        a = _dot(ep.astype(BF16), vp) + _dot(en.astype(BF16), vn)
        return l, a

    l1, a1 = one_map(q1)
    l2, a2 = one_map(q2)
    lam = _diff_lambda(lq1, lk1, lq2, lk2, lam_init)
    o_ref[...] = _finish_heads(a1, l1, a2, l2, lam, sub_ref[...], lam_init).astype(o_ref.dtype)


def _attn_sample(q, k_new, p, v_col, past_k, past_v, layer, row_off, nb, seq, heads, dk, lam_rows,
                 sub, lam_init):
    hw = 2 * dk
    past = past_k.shape[2]
    rb = row_off // seq
    vec = lambda b, h: (0, 0)
    return pl.pallas_call(
        functools.partial(_attn_sample_kernel, dk=dk, lam_init=lam_init),
        out_shape=jax.ShapeDtypeStruct((nb * seq, heads * hw), BF16),
        grid=(nb, heads),
        in_specs=[pl.BlockSpec((seq, hw), lambda b, h: (rb + b, h)),
                  pl.BlockSpec((seq, hw), lambda b, h: (rb + b, h)),
                  pl.BlockSpec((seq, hw), lambda b, h: (rb + b, v_col + h)),
                  pl.BlockSpec((None, None, past, hw), lambda b, h: (layer, b, 0, h)),
                  pl.BlockSpec((None, None, past, hw), lambda b, h: (layer, b, 0, h)),
                  pl.BlockSpec((1, dk), vec), pl.BlockSpec((1, dk), vec),
                  pl.BlockSpec((1, dk), vec), pl.BlockSpec((1, dk), vec),
                  pl.BlockSpec((1, hw), vec)],
        out_specs=pl.BlockSpec((seq, hw), lambda b, h: (b, h)),
        compiler_params=_params("parallel", "parallel"),
        name="attn_sample",
    )(q, k_new, p, past_k, past_v, *lam_rows, sub)


def _conv_kernel(bg_ref, cg_ref, hc_ref, st_ref, cw_ref, y_ref, nc_ref, carry, *, ts):
    s = pl.program_id(2)
    u = cg_ref[...] * hc_ref[...]

    @pl.when(s == 0)
    def _():
        carry[6:8, :] = st_ref[...]

    prev = carry[...]
    p2 = prev[6:7, :]
    p1 = prev[7:8, :]
    row = lax.broadcasted_iota(jnp.int32, u.shape, 0)
    u1 = jnp.where(row == 0, p1, pltpu.roll(u, 1, 0))
    u2 = jnp.where(row == 0, p2, jnp.where(row == 1, p1, pltpu.roll(u, 2, 0)))
    cw = cw_ref[...]
    y = bg_ref[...] * (cw[0:1, :] * u2 + cw[1:2, :] * u1 + cw[2:3, :] * u)
    y_ref[...] = y.astype(y_ref.dtype)
    carry[...] = u[ts - 8:ts, :]

    @pl.when(s == pl.num_programs(2) - 1)
    def _():
        nc_ref[...] = u[ts - 2:ts, :]


def _conv(p, col_b, col_c, col_h, width, row_off, nb, seq, state, cw):
    assert seq >= 8 and state.shape[1] == 2 and cw.shape[0] == 3
    ts = _tile(seq, 512, 8)
    tc = _tile(width, 512, LANES)
    ns = seq // ts
    rb = row_off // ts
    ob, oc, oh = col_b // tc, col_c // tc, col_h // tc
    blk = lambda o: pl.BlockSpec((ts, tc), lambda b, c, s: (rb + b * ns + s, o + c))
    return pl.pallas_call(
        functools.partial(_conv_kernel, ts=ts),
        out_shape=(jax.ShapeDtypeStruct((nb * seq, width), BF16),
                   jax.ShapeDtypeStruct((nb, 2, width), F32)),
        grid=(nb, width // tc, ns),
        in_specs=[blk(ob), blk(oc), blk(oh),
                  pl.BlockSpec((None, 2, tc), lambda b, c, s: (b, 0, c)),
                  pl.BlockSpec((3, tc), lambda b, c, s: (0, c))],
        out_specs=(pl.BlockSpec((ts, tc), lambda b, c, s: (b * ns + s, c)),
                   pl.BlockSpec((None, 2, tc), lambda b, c, s: (b, 0, c))),
        scratch_shapes=[pltpu.VMEM((8, tc), F32)],
        compiler_params=_params("parallel", "parallel", "arbitrary"),
        name="short_conv",
    )(p, p, p, state, cw)


def _outproj_kernel(a1_ref, a2_ref, b1_ref, b2_ref, x_ref, o_ref):
    o_ref[...] = x_ref[...] + (_dot(a1_ref[...], b1_ref[...]) + _dot(a2_ref[...], b2_ref[...]))


def _outproj(a1, a2, w, x):
    m, k1 = a1.shape
    k2 = a2.shape[1]
    assert k1 == k2 and w.shape[0] == k1 + k2
    n = w.shape[1]
    tm = _tile(m, ROW_TILE, 16)
    tn = _tile(n, 512, LANES)
    return pl.pallas_call(
        _outproj_kernel,
        out_shape=jax.ShapeDtypeStruct((m, n), F32),
        grid=(m // tm, n // tn),
        in_specs=[pl.BlockSpec((tm, k1), lambda i, j: (i, 0)),
                  pl.BlockSpec((tm, k2), lambda i, j: (i, 0)),
                  pl.BlockSpec((k1, tn), lambda i, j: (0, j)),
                  pl.BlockSpec((k2, tn), lambda i, j: (1, j)),
                  pl.BlockSpec((tm, tn), lambda i, j: (i, j))],
        out_specs=pl.BlockSpec((tm, tn), lambda i, j: (i, j)),
        compiler_params=_params("parallel", "parallel"),
        name="out_proj",
    )(a1, a2, w, w, x)


def _silu_mul(g, u):
    return (g / (1.0 + jnp.exp(-g))) * u


def _gateup_kernel(a_ref, w1_ref, w3_ref, o_ref):
    a = a_ref[...]
    o_ref[...] = _silu_mul(_dot(a, w1_ref[...]), _dot(a, w3_ref[...])).astype(o_ref.dtype)


def _gateup(a, w1, w3):
    m, k = a.shape
    n = w1.shape[1]
    tm = _tile(m, ROW_TILE, 16)
    tn = _tile(n, 512, LANES)
    return pl.pallas_call(
        _gateup_kernel,
        out_shape=jax.ShapeDtypeStruct((m, n), BF16),
        grid=(m // tm, n // tn),
        in_specs=[pl.BlockSpec((tm, k), lambda i, j: (i, 0)),
                  pl.BlockSpec((k, tn), lambda i, j: (0, j)),
                  pl.BlockSpec((k, tn), lambda i, j: (0, j))],
        out_specs=pl.BlockSpec((tm, tn), lambda i, j: (i, j)),
        compiler_params=_params("parallel", "parallel"),
        name="ffn_gate_up",
    )(a, w1, w3)


def _down_kernel(a_ref, b_ref, x_ref, o_ref):
    d = _dot(a_ref[...], b_ref[...])

    @pl.when(pl.program_id(2) == 0)
    def _():
        o_ref[...] = x_ref[...] + d

    @pl.when(pl.program_id(2) > 0)
    def _():
        o_ref[...] += d


def _down(a, w, x):
    m, k = a.shape
    n = w.shape[1]
    tm = _tile(m, ROW_TILE, 16)
    tn = _tile(n, 1024, LANES)
    tk = _tile(k, 2048, MXU_DIM)
    return pl.pallas_call(
        _down_kernel,
        out_shape=jax.ShapeDtypeStruct((m, n), F32),
        grid=(m // tm, n // tn, k // tk),
        in_specs=[pl.BlockSpec((tm, tk), lambda i, j, kk: (i, kk)),
                  pl.BlockSpec((tk, tn), lambda i, j, kk: (kk, j)),
                  pl.BlockSpec((tm, tn), lambda i, j, kk: (i, j))],
        out_specs=pl.BlockSpec((tm, tn), lambda i, j, kk: (i, j)),
        compiler_params=_params("parallel", "parallel", "arbitrary"),
        name="ffn_down",
    )(a, w, x)


def _grouped_gateup_kernel(te_ref, na_ref, a_ref, w1_ref, w3_ref, o_ref):
    @pl.when(pl.program_id(0) < na_ref[0])
    def _():
        _gateup_kernel(a_ref, w1_ref, w3_ref, o_ref)

    @pl.when(pl.program_id(0) >= na_ref[0])
    def _():
        o_ref[...] = jnp.zeros_like(o_ref)


def _grouped_gateup(xg, w1, w3, tile_expert, n_active, tm):
    r, k = xg.shape
    n = w1.shape[2]
    tn = _tile(n, 512, LANES)
    nj = n // tn
    wspec = pl.BlockSpec((None, k, tn),
                         lambda i, j, te, na: (te[i], 0, jnp.where(i < na[0], j, nj - 1)))
    return pl.pallas_call(
        _grouped_gateup_kernel,
        out_shape=jax.ShapeDtypeStruct((r, n), BF16),
        grid_spec=pltpu.PrefetchScalarGridSpec(
            num_scalar_prefetch=2,
            grid=(r // tm, nj),
            in_specs=[pl.BlockSpec((tm, k), lambda i, j, te, na: (jnp.minimum(i, na[0] - 1), 0)),
                      wspec, wspec],
            out_specs=pl.BlockSpec((tm, tn), lambda i, j, te, na: (i, j))),
        compiler_params=_params("parallel", "parallel"),
        name="moe_gate_up",
    )(tile_expert, n_active, xg, w1, w3)


def _grouped_down_kernel(te_ref, na_ref, a_ref, b_ref, o_ref):
    active = pl.program_id(0) < na_ref[0]
    first = pl.program_id(2) == 0

    @pl.when(active & first)
    def _():
        o_ref[...] = _dot(a_ref[...], b_ref[...])

    @pl.when(active & jnp.logical_not(first))
    def _():
        o_ref[...] += _dot(a_ref[...], b_ref[...])

    @pl.when(jnp.logical_not(active) & first)
    def _():
        o_ref[...] = jnp.zeros_like(o_ref)


def _grouped_down(g, w, tile_expert, n_active, tm):
    r, k = g.shape
    n = w.shape[2]
    tn = _tile(n, 1024, LANES)
    tk = _tile(k, 2048, MXU_DIM)
    nj, nk = n // tn, k // tk

    def a_map(i, j, kk, te, na):
        act = i < na[0]
        return jnp.minimum(i, na[0] - 1), jnp.where(act, kk, nk - 1)

    def b_map(i, j, kk, te, na):
        act = i < na[0]
        return te[i], jnp.where(act, kk, nk - 1), jnp.where(act, j, nj - 1)

    return pl.pallas_call(
        _grouped_down_kernel,
        out_shape=jax.ShapeDtypeStruct((r, n), F32),
        grid_spec=pltpu.PrefetchScalarGridSpec(
            num_scalar_prefetch=2,
            grid=(r // tm, nj, nk),
            in_specs=[pl.BlockSpec((tm, tk), a_map),
                      pl.BlockSpec((None, tk, tn), b_map)],
            out_specs=pl.BlockSpec((tm, tn), lambda i, j, kk, te, na: (i, j))),
        compiler_params=_params("parallel", "parallel", "arbitrary"),
        name="moe_down",
    )(tile_expert, n_active, g, w)


def _router_kernel(x_ref, g_ref, wr_ref, h_ref, r_ref, *, n_exp):
    x = x_ref[...]
    h = (x * lax.rsqrt(jnp.mean(x * x, axis=-1, keepdims=True) + EPS)) * g_ref[...]
    h_ref[...] = h
    logits = jnp.dot(h, wr_ref[...], precision=lax.Precision.HIGHEST, preferred_element_type=F32)
    lane = lax.broadcasted_iota(jnp.int32, logits.shape, 1)
    lanef = lane.astype(F32)
    lg = jnp.where(lane < n_exp, logits, -jnp.inf)
    m1 = jnp.max(lg, axis=-1, keepdims=True)
    i1 = jnp.min(jnp.where(lg == m1, lanef, float(LANES)), axis=-1, keepdims=True)
    lg2 = jnp.where(lanef == i1, -jnp.inf, lg)
    m2 = jnp.max(lg2, axis=-1, keepdims=True)
    i2 = jnp.min(jnp.where(lg2 == m2, lanef, float(LANES)), axis=-1, keepdims=True)
    e = jnp.exp(m2 - m1)
    den = 1.0 + e
    r_ref[...] = jnp.where(lane == 0, i1,
                 jnp.where(lane == 1, i2,
                 jnp.where(lane == 2, 1.0 / den,
                 jnp.where(lane == 3, e / den, 0.0))))


def _router(x, g, w_router):
    t, d = x.shape
    n_exp = w_router.shape[1]
    assert n_exp <= LANES and TOP_K == 2
    wr = jnp.zeros((d, LANES), F32).at[:, :n_exp].set(w_router.astype(F32))
    tr = _tile(t, 256, 8)
    return pl.pallas_call(
        functools.partial(_router_kernel, n_exp=n_exp),
        out_shape=(jax.ShapeDtypeStruct((t, d), F32), jax.ShapeDtypeStruct((t, LANES), F32)),
        grid=(t // tr,),
        in_specs=[pl.BlockSpec((tr, d), lambda i: (i, 0)),
                  pl.BlockSpec((1, d), lambda i: (0, 0)),
                  pl.BlockSpec((d, LANES), lambda i: (0, 0))],
        out_specs=(pl.BlockSpec((tr, d), lambda i: (i, 0)),
                   pl.BlockSpec((tr, LANES), lambda i: (i, 0))),
        compiler_params=_params("parallel"),
        name="ffn_norm_router",
    )(x, g.reshape(1, d), wr)


def _row_copy(src_hbm, idx, dst_vmem, row, sem):
    return pltpu.make_async_copy(src_hbm.at[pl.ds(idx, 1), :], dst_vmem.at[pl.ds(row, 1), :], sem)


def _gather_kernel(na_ref, src_ref, h_hbm, o_ref, buf, sem, *, rows):
    active = pl.program_id(0) < na_ref[0]

    @pl.when(active)
    def _():
        def start(r, c):
            _row_copy(h_hbm, src_ref[0, 0, r], buf, r, sem.at[0]).start()
            return c

        def wait(r, c):
            _row_copy(h_hbm, src_ref[0, 0, r], buf, r, sem.at[0]).wait()
            return c

        lax.fori_loop(0, rows, start, 0)
        lax.fori_loop(0, rows, wait, 0)
        o_ref[...] = buf[...].astype(o_ref.dtype)

    @pl.when(jnp.logical_not(active))
    def _():
        o_ref[...] = jnp.zeros_like(o_ref)


def _gather_rows(h, src, n_active_steps, rows):
    r = src.shape[0]
    d = h.shape[1]
    steps = r // rows
    return pl.pallas_call(
        functools.partial(_gather_kernel, rows=rows),
        out_shape=jax.ShapeDtypeStruct((r, d), BF16),
        grid_spec=pltpu.PrefetchScalarGridSpec(
            num_scalar_prefetch=1,
            grid=(steps,),
            in_specs=[pl.BlockSpec((1, 1, rows), lambda i, na: (i, 0, 0), memory_space=pltpu.SMEM),
                      pl.BlockSpec(memory_space=pl.ANY)],
            out_specs=pl.BlockSpec((rows, d), lambda i, na: (i, 0)),
            scratch_shapes=[pltpu.VMEM((rows, d), F32), pltpu.SemaphoreType.DMA((1,))]),
        compiler_params=_params("arbitrary"),
        name="moe_gather",
    )(n_active_steps, src.reshape(steps, 1, rows), h)


def _combine_kernel(dst_ref, eo_hbm, x_ref, r_ref, o_ref, buf, sem, *, rows):
    def start(r, c):
        _row_copy(eo_hbm, dst_ref[0, 0, r], buf.at[0], r, sem.at[0]).start()
        _row_copy(eo_hbm, dst_ref[0, 0, rows + r], buf.at[1], r, sem.at[0]).start()
        return c

    def wait(r, c):
        _row_copy(eo_hbm, dst_ref[0, 0, r], buf.at[0], r, sem.at[0]).wait()
        _row_copy(eo_hbm, dst_ref[0, 0, rows + r], buf.at[1], r, sem.at[0]).wait()
        return c

    lax.fori_loop(0, rows, start, 0)
    lax.fori_loop(0, rows, wait, 0)
    route = r_ref[...]
    o_ref[...] = x_ref[...] + (route[:, 2:3] * buf[0] + route[:, 3:4] * buf[1])


def _combine(eo, dst, x, route, rows):
    t, d = x.shape
    steps = t // rows
    return pl.pallas_call(
        functools.partial(_combine_kernel, rows=rows),
        out_shape=jax.ShapeDtypeStruct((t, d), F32),
        grid=(steps,),
        in_specs=[pl.BlockSpec((1, 1, 2 * rows), lambda i: (i, 0, 0), memory_space=pltpu.SMEM),
                  pl.BlockSpec(memory_space=pl.ANY),
                  pl.BlockSpec((rows, d), lambda i: (i, 0)),
                  pl.BlockSpec((rows, LANES), lambda i: (i, 0))],
        out_specs=pl.BlockSpec((rows, d), lambda i: (i, 0)),
        scratch_shapes=[pltpu.VMEM((2, rows, d), F32), pltpu.SemaphoreType.DMA((1,))],
        compiler_params=_params("arbitrary"),
        name="moe_combine",
    )(dst, eo, x, route)


def _moe_ffn(x, g, w_router, w1, w3, w2):
    t, d = x.shape
    n_exp = w_router.shape[1]
    tm = EXPERT_ROW_TILE
    rows_g = _tile(tm, GATHER_ROWS, 8)
    rows_c = _tile(t, GATHER_ROWS, 8)
    n_tiles = -(-(TOP_K * t) // tm) + n_exp
    r = n_tiles * tm

    h, route = _router(x, g, w_router)

    e_flat = jnp.concatenate([route[:, 0], route[:, 1]]).astype(jnp.int32)
    onehot = (e_flat[:, None] == jnp.arange(n_exp, dtype=jnp.int32)[None, :]).astype(jnp.int32)
    rank = jnp.sum((jnp.cumsum(onehot, axis=0) - onehot) * onehot, axis=1)
    counts = jnp.sum(onehot, axis=0)
    tiles_per = (counts + tm - 1) // tm
    tile_end = jnp.cumsum(tiles_per)
    tile_start = tile_end - tiles_per
    n_active = tile_end[-1]
    dst = tile_start[e_flat] * tm + rank
    tile_ids = jnp.arange(n_tiles, dtype=jnp.int32)
    tile_expert = jnp.minimum(jnp.searchsorted(tile_end, tile_ids, side="right"), n_exp - 1)
    last_expert = tile_expert[jnp.maximum(n_active - 1, 0)]
    tile_expert = jnp.where(tile_ids < n_active, tile_expert, last_expert).astype(jnp.int32)
    order = jnp.argsort(e_flat, stable=True).astype(jnp.int32)
    group_off = jnp.cumsum(counts) - counts
    slot = jnp.arange(r, dtype=jnp.int32)
    slot_e = tile_expert[slot // tm]
    within = slot - tile_start[slot_e] * tm
    valid = (within < counts[slot_e]) & (slot // tm < n_active)
    src = jnp.where(valid, order[jnp.clip(group_off[slot_e] + within, 0, TOP_K * t - 1)] % t, 0)
    src = src.astype(jnp.int32)

    na = n_active.astype(jnp.int32).reshape(1)
    xg = _gather_rows(h, src, na * (tm // rows_g), rows_g)
    gu = _grouped_gateup(xg, w1, w3, tile_expert, na, tm)
    eo = _grouped_down(gu, w2, tile_expert, na, tm)
    steps = t // rows_c
    dst2 = jnp.concatenate([dst[:t].reshape(steps, 1, rows_c), dst[t:].reshape(steps, 1, rows_c)],
                           axis=2).astype(jnp.int32)
    return _combine(eo, dst2, x, route, rows_c)


def kernel(x_prompt, x_sample, cache_k, cache_v, cache_conv, w_in, w_out, conv_w, attn_norm,
           q_norm, k_norm, lambda_q1, lambda_k1, lambda_q2, lambda_k2, subln, ffn_norm,
           w1_dense, w3_dense, w2_dense, w_router, w1_exp, w3_exp, w2_exp):
    depth = w_in.shape[0]
    nbp, seq, d = x_prompt.shape
    nbs, sseq, _ = x_sample.shape
    past, heads, _, dk = cache_k.shape[2:]
    dv = cache_v.shape[-1]
    cdim = cache_conv.shape[-1]
    hw = 2 * dk
    assert dv == hw == LANES, "one head (both maps / the value row) must span one lane tile"
    qk_dim = heads * hw
    attn_dim = heads * dv
    tp, ts = nbp * seq, nbs * sseq
    assert tp % sseq == 0 and seq % CHUNK == 0
    c_q, c_k, c_v = 0, qk_dim, 2 * qk_dim
    c_b = c_v + attn_dim
    c_c, c_h = c_b + cdim, c_b + 2 * cdim

    x = jnp.concatenate([x_prompt.reshape(tp, d), x_sample.reshape(ts, d)], axis=0)
    zero_state = jnp.zeros((nbp, 2, cdim), F32)
    ck = cache_k.reshape(depth, nbs, past, qk_dim)
    cv = cache_v.reshape(depth, nbs, past, attn_dim)

    ks, vs, convs_p, convs_s = [], [], [], []
    for l in range(depth):
        lam0 = _lambda_init(l)
        lam_rows = [a[l].reshape(1, dk) for a in (lambda_q1, lambda_k1, lambda_q2, lambda_k2)]
        sub = subln[l].reshape(1, dv)

        h = _rmsnorm(x, attn_norm[l])
        p = _matmul(h, w_in[l].astype(BF16), F32)
        q = _groupnorm(p, c_q, qk_dim, q_norm[l], dk, dk ** -0.5, BF16)
        k = _groupnorm(p, c_k, qk_dim, k_norm[l], dk, 1.0, F32)
        o_p = _attn_prompt(q, k, 0, p, c_v // hw, nbp, seq, heads, dk, lam_rows, sub, lam0)
        o_s = _attn_sample(q, k, p, c_v // hw, ck, cv, l, tp, nbs, sseq, heads, dk, lam_rows,
                           sub, lam0)
        y_p, nc_p = _conv(p, c_b, c_c, c_h, cdim, 0, nbp, seq, zero_state, conv_w[l])
        y_s, nc_s = _conv(p, c_b, c_c, c_h, cdim, tp, nbs, sseq, cache_conv[l], conv_w[l])
        x = _outproj(jnp.concatenate([o_p, o_s], axis=0), jnp.concatenate([y_p, y_s], axis=0),
                     w_out[l].astype(BF16), x)
        ks.append(k)
        vs.append(p[:, c_v:c_v + attn_dim])
        convs_p.append(nc_p)
        convs_s.append(nc_s)

        m = l // 2
        if l % 2 == 0:
            hf = _rmsnorm(x, ffn_norm[l])
            gu = _gateup(hf, w1_dense[m].astype(BF16), w3_dense[m].astype(BF16))
            x = _down(gu, w2_dense[m].astype(BF16), x)
        else:
            x = _moe_ffn(x, ffn_norm[l], w_router[m], w1_exp[m].astype(BF16),
                         w3_exp[m].astype(BF16), w2_exp[m].astype(BF16))

    k_all = jnp.stack(ks)
    v_all = jnp.stack(vs)
    return (x[:tp].reshape(nbp, seq, d),
            x[tp:].reshape(nbs, sseq, d),
            k_all[:, :tp].reshape(depth, nbp, seq, heads, 2, dk),
            v_all[:, :tp].reshape(depth, nbp, seq, heads, dv),
            jnp.stack(convs_p),
            k_all[:, tp:].reshape(depth, nbs, sseq, heads, 2, dk),
            v_all[:, tp:].reshape(depth, nbs, sseq, heads, dv),
            jnp.stack(convs_s))
```

```python
import functools
import math

import jax
import jax.numpy as jnp
from jax import lax
from jax.experimental import pallas as pl
from jax.experimental.pallas import tpu as pltpu

CHUNK = 64
TOP_K = 2
EPS = 1e-6
NEG_INF = -1e30
F32 = jnp.float32
BF16 = jnp.bfloat16

LANES = 128
MXU_DIM = 256
VMEM_LIMIT = 56 * 1024 * 1024

ROW_TILE = 1100
EXPERT_ROW_TILE = 1024
EXPERT_SUB_ROWS = 256
GATHER_ROWS = 256
ATTN_BLOCK = 512


def _lambda_init(l):
    return 0.8 - 0.6 * math.exp(-0.3 * l)


def _tile(n, pref, mult):
    if n <= pref:
        return n
    best = None
    for t in range(mult, pref + 1, mult):
        if n % t == 0:
            best = t
    assert best is not None, (n, pref, mult)
    return best


def _params(*sem):
    return pltpu.CompilerParams(dimension_semantics=sem, vmem_limit_bytes=VMEM_LIMIT)


def _dot(a, b):
    return jnp.dot(a, b, preferred_element_type=F32)


def _rmsnorm_kernel(x_ref, g_ref, o_ref):
    x = x_ref[...]
    y = x * lax.rsqrt(jnp.mean(x * x, axis=-1, keepdims=True) + EPS)
    o_ref[...] = (y * g_ref[...]).astype(o_ref.dtype)


def _rmsnorm(x, g):
    t, d = x.shape
    tr = _tile(t, 256, 16)
    return pl.pallas_call(
        _rmsnorm_kernel,
        out_shape=jax.ShapeDtypeStruct((t, d), BF16),
        grid=(t // tr,),
        in_specs=[pl.BlockSpec((tr, d), lambda i: (i, 0)),
                  pl.BlockSpec((1, d), lambda i: (0, 0))],
        out_specs=pl.BlockSpec((tr, d), lambda i: (i, 0)),
        compiler_params=_params("parallel"),
        name="rmsnorm",
    )(x, g.reshape(1, d))


def _mm_kernel(a_ref, b_ref, o_ref):
    o_ref[...] = _dot(a_ref[...], b_ref[...]).astype(o_ref.dtype)


def _matmul(a, b, out_dtype, tn_pref=1024):
    m, k = a.shape
    n = b.shape[1]
    tm = _tile(m, ROW_TILE, 16)
    tn = _tile(n, tn_pref, LANES)
    return pl.pallas_call(
        _mm_kernel,
        out_shape=jax.ShapeDtypeStruct((m, n), out_dtype),
        grid=(m // tm, n // tn),
        in_specs=[pl.BlockSpec((tm, k), lambda i, j: (i, 0)),
                  pl.BlockSpec((k, tn), lambda i, j: (0, j))],
        out_specs=pl.BlockSpec((tm, tn), lambda i, j: (i, j)),
        compiler_params=_params("parallel", "parallel"),
        name="in_proj",
    )(a, b)


def _groupnorm_kernel(p_ref, g_ref, gm_ref, o_ref, *, group, scale):
    x = p_ref[...]
    sq = x * x
    hi = sq.astype(BF16)
    lo = (sq - hi.astype(F32)).astype(BF16)
    gm = gm_ref[...]
    ssq = _dot(hi, gm) + _dot(lo, gm)
    y = x * lax.rsqrt(ssq * (1.0 / group) + EPS)
    o_ref[...] = ((y * g_ref[...]) * scale).astype(o_ref.dtype)


def _groupnorm(p, col_off, width, gain, group, scale, out_dtype):
    t = p.shape[0]
    tw = MXU_DIM
    assert width % tw == 0 and col_off % tw == 0 and tw % group == 0
    tr = _tile(t, ROW_TILE, 16)
    ids = jnp.arange(tw) // group
    gm = (ids[:, None] == ids[None, :]).astype(BF16)
    g = jnp.tile(gain.astype(F32), tw // group).reshape(1, tw)
    off = col_off // tw
    return pl.pallas_call(
        functools.partial(_groupnorm_kernel, group=group, scale=scale),
        out_shape=jax.ShapeDtypeStruct((t, width), out_dtype),
        grid=(t // tr, width // tw),
        in_specs=[pl.BlockSpec((tr, tw), lambda i, j: (i, j + off)),
                  pl.BlockSpec((1, tw), lambda i, j: (0, 0)),
                  pl.BlockSpec((tw, tw), lambda i, j: (0, 0))],
        out_specs=pl.BlockSpec((tr, tw), lambda i, j: (i, j)),
        compiler_params=_params("parallel", "parallel"),
        name="qk_norm",
    )(p, g, gm)


def _diff_lambda(lq1, lk1, lq2, lk2, lam_init):
    a = jnp.exp(jnp.sum(lq1[...] * lk1[...], axis=-1, keepdims=True))
    b = jnp.exp(jnp.sum(lq2[...] * lk2[...], axis=-1, keepdims=True))
    return a - b + lam_init


def _finish_heads(a1, l1, a2, l2, lam, sub, lam_init):
    o = a1 / l1 - lam * (a2 / l2)
    y = o * lax.rsqrt(jnp.mean(o * o, axis=-1, keepdims=True) + EPS)
    return (y * sub) * (1.0 - lam_init)


def _split_maps(q, dk):
    lane = lax.broadcasted_iota(jnp.int32, q.shape, 1)
    zero = jnp.zeros_like(q)
    return jnp.where(lane < dk, q, zero), jnp.where(lane >= dk, q, zero)


def _qk(q, k):
    return lax.dot_general(q, k, (((1,), (1,)), ((), ())), preferred_element_type=F32)


def _attn_prompt_kernel(q_ref, k_ref, v_ref, lq1, lk1, lq2, lk2, sub_ref, o_ref,
                        kb, vt, qt, m_s, l_s, acc, s_ref, p_ref, *, tq, dk, lam_init):
    qi = pl.program_id(2)
    nq = kb.shape[0]

    @pl.when(qi == 0)
    def _():
        for jj in range(nq):
            kb[jj] = k_ref[jj * tq:(jj + 1) * tq, :].astype(BF16)
            vt[jj] = v_ref[jj * tq:(jj + 1) * tq, :].T.astype(BF16)

    q = q_ref[...].astype(F32)
    lane = lax.broadcasted_iota(jnp.int32, q.shape, 1)
    qt[:, 0:tq] = jnp.where(lane < dk, q, 0.0).T.astype(BF16)
    qt[:, tq:2 * tq] = jnp.where(lane >= dk, q, 0.0).T.astype(BF16)
    m_s[...] = jnp.full(m_s.shape, NEG_INF, F32)
    l_s[...] = jnp.zeros(l_s.shape, F32)
    acc[...] = jnp.zeros(acc.shape, F32)

    def block(j, masked):
        kblk = kb[j]
        vblk = vt[j]
        for lo in (0, tq):
            s_ref[:, lo:lo + tq] = _dot(kblk, qt[:, lo:lo + tq])
        for lo in (0, tq):
            for c in range(lo, lo + tq, LANES):
                s = s_ref[:, c:c + LANES]
                if masked:
                    key = lax.broadcasted_iota(jnp.int32, s.shape, 0)
                    qry = lax.broadcasted_iota(jnp.int32, s.shape, 1) + (c - lo)
                    s = jnp.where(key // CHUNK <= qry // CHUNK, s, NEG_INF)
                m_old = m_s[:, c:c + LANES]
                m_new = jnp.maximum(m_old, jnp.max(s, axis=0, keepdims=True))
                alpha = jnp.exp(m_old - m_new)
                p = jnp.exp(s - m_new)
                l_s[:, c:c + LANES] = (alpha * l_s[:, c:c + LANES]
                                       + jnp.sum(p, axis=0, keepdims=True))
                m_s[:, c:c + LANES] = m_new
                p_ref[:, c:c + LANES] = p.astype(BF16)
                acc[:, c:c + LANES] = alpha * acc[:, c:c + LANES]
            acc[:, lo:lo + tq] += _dot(vblk, p_ref[:, lo:lo + tq])

    def body(j, c):
        block(j, False)
        return c

    lax.fori_loop(0, qi, body, 0)
    block(qi, True)
    lam = _diff_lambda(lq1, lk1, lq2, lk2, lam_init)
    r = acc[...] / l_s[...]
    o = (r[:, 0:tq] - lam * r[:, tq:2 * tq]).T
    y = o * lax.rsqrt(jnp.mean(o * o, axis=-1, keepdims=True) + EPS)
    o_ref[...] = ((y * sub_ref[...]) * (1.0 - lam_init)).astype(o_ref.dtype)


def _attn_prompt(q, kv_src, k_col, v_src, v_col, nb, seq, heads, dk, lam_rows, sub, lam_init):
    hw = 2 * dk
    tq = _tile(seq, ATTN_BLOCK, LANES)
    nq = seq // tq
    vec = lambda i, h, j: (0, 0)
    return pl.pallas_call(
        functools.partial(_attn_prompt_kernel, tq=tq, dk=dk, lam_init=lam_init),
        out_shape=jax.ShapeDtypeStruct((nb * seq, heads * hw), BF16),
        grid=(nb, heads, nq),
        in_specs=[pl.BlockSpec((tq, hw), lambda b, h, j: (b * nq + j, h)),
                  pl.BlockSpec((seq, hw), lambda b, h, j: (b, k_col + h)),
                  pl.BlockSpec((seq, hw), lambda b, h, j: (b, v_col + h)),
                  pl.BlockSpec((1, dk), vec), pl.BlockSpec((1, dk), vec),
                  pl.BlockSpec((1, dk), vec), pl.BlockSpec((1, dk), vec),
                  pl.BlockSpec((1, hw), vec)],
        out_specs=pl.BlockSpec((tq, hw), lambda b, h, j: (b * nq + j, h)),
        scratch_shapes=[pltpu.VMEM((nq, tq, hw), BF16), pltpu.VMEM((nq, hw, tq), BF16),
                        pltpu.VMEM((hw, 2 * tq), BF16), pltpu.VMEM((1, 2 * tq), F32),
                        pltpu.VMEM((1, 2 * tq), F32), pltpu.VMEM((hw, 2 * tq), F32),
                        pltpu.VMEM((tq, 2 * tq), F32), pltpu.VMEM((tq, 2 * tq), BF16)],
        compiler_params=_params("parallel", "parallel", "arbitrary"),
        name="attn_prompt",
    )(q, kv_src, v_src, *lam_rows, sub)


def _attn_sample_kernel(q_ref, kn_ref, vn_ref, kp_ref, vp_ref, lq1, lk1, lq2, lk2, sub_ref, o_ref,
                        *, dk, lam_init):
    q1, q2 = _split_maps(q_ref[...], dk)
    kp = kp_ref[...].astype(BF16)
    vp = vp_ref[...].astype(BF16)
    kn = kn_ref[...].astype(BF16)
    vn = vn_ref[...].astype(BF16)

    def one_map(q):
        sp = _qk(q, kp)
        sn = _qk(q, kn)
        m = jnp.maximum(jnp.max(sp, axis=-1, keepdims=True), jnp.max(sn, axis=-1, keepdims=True))
        ep = jnp.exp(sp - m)
        en = jnp.exp(sn - m)
        l = jnp.sum(ep, axis=-1, keepdims=True) + jnp.sum(en, axis=-1, keepdims=True)
        a = _dot(ep.astype(BF16), vp) + _dot(en.astype(BF16), vn)
        return l, a

    l1, a1 = one_map(q1)
    l2, a2 = one_map(q2)
    lam = _diff_lambda(lq1, lk1, lq2, lk2, lam_init)
    o_ref[...] = _finish_heads(a1, l1, a2, l2, lam, sub_ref[...], lam_init).astype(o_ref.dtype)


def _attn_sample(q, k_new, p, v_col, past_k, past_v, layer, row_off, nb, seq, heads, dk, lam_rows,
                 sub, lam_init):
    hw = 2 * dk
    past = past_k.shape[2]
    rb = row_off // seq
    vec = lambda b, h: (0, 0)
    return pl.pallas_call(
        functools.partial(_attn_sample_kernel, dk=dk, lam_init=lam_init),
        out_shape=jax.ShapeDtypeStruct((nb * seq, heads * hw), BF16),
        grid=(nb, heads),
        in_specs=[pl.BlockSpec((seq, hw), lambda b, h: (rb + b, h)),
                  pl.BlockSpec((seq, hw), lambda b, h: (rb + b, h)),
                  pl.BlockSpec((seq, hw), lambda b, h: (rb + b, v_col + h)),
                  pl.BlockSpec((None, None, past, hw), lambda b, h: (layer, b, 0, h)),
                  pl.BlockSpec((None, None, past, hw), lambda b, h: (layer, b, 0, h)),
                  pl.BlockSpec((1, dk), vec), pl.BlockSpec((1, dk), vec),
                  pl.BlockSpec((1, dk), vec), pl.BlockSpec((1, dk), vec),
                  pl.BlockSpec((1, hw), vec)],
        out_specs=pl.BlockSpec((seq, hw), lambda b, h: (b, h)),
        compiler_params=_params("parallel", "parallel"),
        name="attn_sample",
    )(q, k_new, p, past_k, past_v, *lam_rows, sub)


def _conv_kernel(bg_ref, cg_ref, hc_ref, st_ref, cw_ref, y_ref, nc_ref, carry, *, ts):
    s = pl.program_id(2)
    u = cg_ref[...] * hc_ref[...]

    @pl.when(s == 0)
    def _():
        carry[6:8, :] = st_ref[...]

    prev = carry[...]
    p2 = prev[6:7, :]
    p1 = prev[7:8, :]
    row = lax.broadcasted_iota(jnp.int32, u.shape, 0)
    u1 = jnp.where(row == 0, p1, pltpu.roll(u, 1, 0))
    u2 = jnp.where(row == 0, p2, jnp.where(row == 1, p1, pltpu.roll(u, 2, 0)))
    cw = cw_ref[...]
    y = bg_ref[...] * (cw[0:1, :] * u2 + cw[1:2, :] * u1 + cw[2:3, :] * u)
    y_ref[...] = y.astype(y_ref.dtype)
    carry[...] = u[ts - 8:ts, :]

    @pl.when(s == pl.num_programs(2) - 1)
    def _():
        nc_ref[...] = u[ts - 2:ts, :]


def _conv(p, col_b, col_c, col_h, width, row_off, nb, seq, state, cw):
    assert seq >= 8 and state.shape[1] == 2 and cw.shape[0] == 3
    ts = _tile(seq, 512, 8)
    tc = _tile(width, 512, LANES)
    ns = seq // ts
    rb = row_off // ts
    ob, oc, oh = col_b // tc, col_c // tc, col_h // tc
    blk = lambda o: pl.BlockSpec((ts, tc), lambda b, c, s: (rb + b * ns + s, o + c))
    return pl.pallas_call(
        functools.partial(_conv_kernel, ts=ts),
        out_shape=(jax.ShapeDtypeStruct((nb * seq, width), BF16),
                   jax.ShapeDtypeStruct((nb, 2, width), F32)),
        grid=(nb, width // tc, ns),
        in_specs=[blk(ob), blk(oc), blk(oh),
                  pl.BlockSpec((None, 2, tc), lambda b, c, s: (b, 0, c)),
                  pl.BlockSpec((3, tc), lambda b, c, s: (0, c))],
        out_specs=(pl.BlockSpec((ts, tc), lambda b, c, s: (b * ns + s, c)),
                   pl.BlockSpec((None, 2, tc), lambda b, c, s: (b, 0, c))),
        scratch_shapes=[pltpu.VMEM((8, tc), F32)],
        compiler_params=_params("parallel", "parallel", "arbitrary"),
        name="short_conv",
    )(p, p, p, state, cw)


def _outproj_kernel(a1_ref, a2_ref, b1_ref, b2_ref, x_ref, o_ref):
    o_ref[...] = x_ref[...] + (_dot(a1_ref[...], b1_ref[...]) + _dot(a2_ref[...], b2_ref[...]))


def _outproj(a1, a2, w, x):
    m, k1 = a1.shape
    k2 = a2.shape[1]
    assert k1 == k2 and w.shape[0] == k1 + k2
    n = w.shape[1]
    tm = _tile(m, ROW_TILE, 16)
    tn = _tile(n, 512, LANES)
    return pl.pallas_call(
        _outproj_kernel,
        out_shape=jax.ShapeDtypeStruct((m, n), F32),
        grid=(m // tm, n // tn),
        in_specs=[pl.BlockSpec((tm, k1), lambda i, j: (i, 0)),
                  pl.BlockSpec((tm, k2), lambda i, j: (i, 0)),
                  pl.BlockSpec((k1, tn), lambda i, j: (0, j)),
                  pl.BlockSpec((k2, tn), lambda i, j: (1, j)),
                  pl.BlockSpec((tm, tn), lambda i, j: (i, j))],
        out_specs=pl.BlockSpec((tm, tn), lambda i, j: (i, j)),
        compiler_params=_params("parallel", "parallel"),
        name="out_proj",
    )(a1, a2, w, w, x)


def _silu_mul(g, u):
    return (g / (1.0 + jnp.exp(-g))) * u


def _gateup_kernel(a_ref, w1_ref, w3_ref, o_ref):
    a = a_ref[...]
    o_ref[...] = _silu_mul(_dot(a, w1_ref[...]), _dot(a, w3_ref[...])).astype(o_ref.dtype)


def _gateup(a, w1, w3):
    m, k = a.shape
    n = w1.shape[1]
    tm = _tile(m, ROW_TILE, 16)
    tn = _tile(n, 512, LANES)
    return pl.pallas_call(
        _gateup_kernel,
        out_shape=jax.ShapeDtypeStruct((m, n), BF16),
        grid=(m // tm, n // tn),
        in_specs=[pl.BlockSpec((tm, k), lambda i, j: (i, 0)),
                  pl.BlockSpec((k, tn), lambda i, j: (0, j)),
                  pl.BlockSpec((k, tn), lambda i, j: (0, j))],
        out_specs=pl.BlockSpec((tm, tn), lambda i, j: (i, j)),
        compiler_params=_params("parallel", "parallel"),
        name="ffn_gate_up",
    )(a, w1, w3)


def _down_kernel(a_ref, b_ref, x_ref, o_ref):
    d = _dot(a_ref[...], b_ref[...])

    @pl.when(pl.program_id(2) == 0)
    def _():
        o_ref[...] = x_ref[...] + d

    @pl.when(pl.program_id(2) > 0)
    def _():
        o_ref[...] += d


def _down(a, w, x):
    m, k = a.shape
    n = w.shape[1]
    tm = _tile(m, ROW_TILE, 16)
    tn = _tile(n, 1024, LANES)
    tk = _tile(k, 2048, MXU_DIM)
    return pl.pallas_call(
        _down_kernel,
        out_shape=jax.ShapeDtypeStruct((m, n), F32),
        grid=(m // tm, n // tn, k // tk),
        in_specs=[pl.BlockSpec((tm, tk), lambda i, j, kk: (i, kk)),
                  pl.BlockSpec((tk, tn), lambda i, j, kk: (kk, j)),
                  pl.BlockSpec((tm, tn), lambda i, j, kk: (i, j))],
        out_specs=pl.BlockSpec((tm, tn), lambda i, j, kk: (i, j)),
        compiler_params=_params("parallel", "parallel", "arbitrary"),
        name="ffn_down",
    )(a, w, x)


def _grouped_gateup_kernel(te_ref, na_ref, rows_ref, a_ref, w1_ref, w3_ref, o_ref, *, sub):
    rows = rows_ref[pl.program_id(0)]
    tm = a_ref.shape[0]

    def compute(lo, size):
        a = a_ref[lo:lo + size, :]
        g = _dot(a, w1_ref[...].astype(BF16))
        u = _dot(a, w3_ref[...].astype(BF16))
        o_ref[lo:lo + size, :] = _silu_mul(g, u).astype(o_ref.dtype)

    @pl.when(rows == tm)
    def _():
        compute(0, tm)

    for s in range(tm // sub):
        @pl.when((rows < tm) & (s * sub < rows))
        def _():
            compute(s * sub, sub)

        @pl.when((rows < tm) & (s * sub >= rows))
        def _():
            o_ref[s * sub:(s + 1) * sub, :] = jnp.zeros((sub, o_ref.shape[1]), o_ref.dtype)


def _grouped_gateup(xg, w1, w3, layer, tile_expert, n_active, tile_rows, tm):
    r, k = xg.shape
    n = w1.shape[3]
    tn = _tile(n, 256, LANES)
    nj = n // tn
    sub = _tile(tm, EXPERT_SUB_ROWS, 16)
    wspec = pl.BlockSpec(
        (None, None, k, tn),
        lambda i, j, te, na, tr: (layer, te[i], 0, jnp.where(i < na[0], j, nj - 1)))
    return pl.pallas_call(
        functools.partial(_grouped_gateup_kernel, sub=sub),
        out_shape=jax.ShapeDtypeStruct((r, n), BF16),
        grid_spec=pltpu.PrefetchScalarGridSpec(
            num_scalar_prefetch=3,
            grid=(r // tm, nj),
            in_specs=[pl.BlockSpec((tm, k),
                                   lambda i, j, te, na, tr: (jnp.minimum(i, na[0] - 1), 0)),
                      wspec, wspec],
            out_specs=pl.BlockSpec((tm, tn), lambda i, j, te, na, tr: (i, j))),
        compiler_params=_params("parallel", "parallel"),
        name="moe_gate_up",
    )(tile_expert, n_active, tile_rows, xg, w1, w3)


def _grouped_down_kernel(te_ref, na_ref, rows_ref, a_ref, b_ref, o_ref, *, sub):
    rows = rows_ref[pl.program_id(0)]
    first = pl.program_id(2) == 0
    tm = a_ref.shape[0]

    def compute(lo, size):
        d = _dot(a_ref[lo:lo + size, :], b_ref[...])

        @pl.when(first)
        def _():
            o_ref[lo:lo + size, :] = d

        @pl.when(jnp.logical_not(first))
        def _():
            o_ref[lo:lo + size, :] += d

    @pl.when(rows == tm)
    def _():
        compute(0, tm)

    for s in range(tm // sub):
        @pl.when((rows < tm) & (s * sub < rows))
        def _():
            compute(s * sub, sub)

        @pl.when((rows < tm) & (s * sub >= rows) & first)
        def _():
            o_ref[s * sub:(s + 1) * sub, :] = jnp.zeros((sub, o_ref.shape[1]), o_ref.dtype)


def _grouped_down(g, w, tile_expert, n_active, tile_rows, tm):
    r, k = g.shape
    n = w.shape[2]
    tn = _tile(n, 1024, LANES)
    tk = _tile(k, 2048, MXU_DIM)
    nj, nk = n // tn, k // tk
    sub = _tile(tm, EXPERT_SUB_ROWS, 16)

    def a_map(i, j, kk, te, na, tr):
        act = i < na[0]
        return jnp.minimum(i, na[0] - 1), jnp.where(act, kk, nk - 1)

    def b_map(i, j, kk, te, na, tr):
        act = i < na[0]
        return te[i], jnp.where(act, kk, nk - 1), jnp.where(act, j, nj - 1)

    return pl.pallas_call(
        functools.partial(_grouped_down_kernel, sub=sub),
        out_shape=jax.ShapeDtypeStruct((r, n), F32),
        grid_spec=pltpu.PrefetchScalarGridSpec(
            num_scalar_prefetch=3,
            grid=(r // tm, nj, nk),
            in_specs=[pl.BlockSpec((tm, tk), a_map),
                      pl.BlockSpec((None, tk, tn), b_map)],
            out_specs=pl.BlockSpec((tm, tn), lambda i, j, kk, te, na, tr: (i, j))),
        compiler_params=_params("parallel", "parallel", "arbitrary"),
        name="moe_down",
    )(tile_expert, n_active, tile_rows, g, w)


def _router_kernel(x_ref, g_ref, wr_ref, h_ref, r_ref, *, n_exp):
    x = x_ref[...]
    h = (x * lax.rsqrt(jnp.mean(x * x, axis=-1, keepdims=True) + EPS)) * g_ref[...]
    h_ref[...] = h
    logits = jnp.dot(h, wr_ref[...], precision=lax.Precision.HIGHEST, preferred_element_type=F32)
    lane = lax.broadcasted_iota(jnp.int32, logits.shape, 1)
    lanef = lane.astype(F32)
    lg = jnp.where(lane < n_exp, logits, -jnp.inf)
    m1 = jnp.max(lg, axis=-1, keepdims=True)
    i1 = jnp.min(jnp.where(lg == m1, lanef, float(LANES)), axis=-1, keepdims=True)
    lg2 = jnp.where(lanef == i1, -jnp.inf, lg)
    m2 = jnp.max(lg2, axis=-1, keepdims=True)
    i2 = jnp.min(jnp.where(lg2 == m2, lanef, float(LANES)), axis=-1, keepdims=True)
    e = jnp.exp(m2 - m1)
    den = 1.0 + e
    r_ref[...] = jnp.where(lane == 0, i1,
                 jnp.where(lane == 1, i2,
                 jnp.where(lane == 2, 1.0 / den,
                 jnp.where(lane == 3, e / den, 0.0))))


def _router(x, g, w_router):
    t, d = x.shape
    n_exp = w_router.shape[1]
    assert n_exp <= LANES and TOP_K == 2
    wr = jnp.zeros((d, LANES), F32).at[:, :n_exp].set(w_router.astype(F32))
    tr = _tile(t, 256, 8)
    return pl.pallas_call(
        functools.partial(_router_kernel, n_exp=n_exp),
        out_shape=(jax.ShapeDtypeStruct((t, d), F32), jax.ShapeDtypeStruct((t, LANES), F32)),
        grid=(t // tr,),
        in_specs=[pl.BlockSpec((tr, d), lambda i: (i, 0)),
                  pl.BlockSpec((1, d), lambda i: (0, 0)),
                  pl.BlockSpec((d, LANES), lambda i: (0, 0))],
        out_specs=(pl.BlockSpec((tr, d), lambda i: (i, 0)),
                   pl.BlockSpec((tr, LANES), lambda i: (i, 0))),
        compiler_params=_params("parallel"),
        name="ffn_norm_router",
    )(x, g.reshape(1, d), wr)


def _row_copy(src_hbm, idx, dst_vmem, row, sem):
    return pltpu.make_async_copy(src_hbm.at[pl.ds(idx, 1), :], dst_vmem.at[pl.ds(row, 1), :], sem)


def _gather_kernel(na_ref, src_ref, nxt_ref, h_hbm, o_ref, buf, sem, *, rows):
    i = pl.program_id(0)
    slot = i % 2

    def issue(idx_ref, sl):
        def body(r, c):
            _row_copy(h_hbm, idx_ref[0, 0, r], buf.at[sl], r, sem.at[sl]).start()
            return c
        lax.fori_loop(0, rows, body, 0)

    @pl.when(i == 0)
    def _():
        issue(src_ref, 0)

    @pl.when(i + 1 < na_ref[0])
    def _():
        issue(nxt_ref, 1 - slot)

    @pl.when(i < na_ref[0])
    def _():
        def wait(r, c):
            _row_copy(h_hbm, src_ref[0, 0, r], buf.at[slot], r, sem.at[slot]).wait()
            return c
        lax.fori_loop(0, rows, wait, 0)
        o_ref[...] = buf[slot].astype(o_ref.dtype)

    @pl.when(i >= na_ref[0])
    def _():
        o_ref[...] = jnp.zeros_like(o_ref)


def _gather_rows(h, src, n_active_steps, rows):
    r = src.shape[0]
    d = h.shape[1]
    steps = r // rows
    src3 = src.reshape(steps, 1, rows)
    idx_spec = lambda f: pl.BlockSpec((1, 1, rows), f, memory_space=pltpu.SMEM)
    return pl.pallas_call(
        functools.partial(_gather_kernel, rows=rows),
        out_shape=jax.ShapeDtypeStruct((r, d), BF16),
        grid_spec=pltpu.PrefetchScalarGridSpec(
            num_scalar_prefetch=1,
            grid=(steps,),
            in_specs=[idx_spec(lambda i, na: (i, 0, 0)),
                      idx_spec(lambda i, na: (jnp.minimum(i + 1, steps - 1), 0, 0)),
                      pl.BlockSpec(memory_space=pl.ANY)],
            out_specs=pl.BlockSpec((rows, d), lambda i, na: (i, 0)),
            scratch_shapes=[pltpu.VMEM((2, rows, d), F32), pltpu.SemaphoreType.DMA((2,))]),
        compiler_params=_params("arbitrary"),
        name="moe_gather",
    )(n_active_steps, src3, src3, h)


def _combine_kernel(dst_ref, eo_hbm, x_ref, r_ref, o_ref, buf, sem, *, rows):
    def start(r, c):
        _row_copy(eo_hbm, dst_ref[0, 0, r], buf.at[0], r, sem.at[0]).start()
        _row_copy(eo_hbm, dst_ref[0, 0, rows + r], buf.at[1], r, sem.at[0]).start()
        return c

    def wait(r, c):
        _row_copy(eo_hbm, dst_ref[0, 0, r], buf.at[0], r, sem.at[0]).wait()
        _row_copy(eo_hbm, dst_ref[0, 0, rows + r], buf.at[1], r, sem.at[0]).wait()
        return c

    lax.fori_loop(0, rows, start, 0)
    lax.fori_loop(0, rows, wait, 0)
    route = r_ref[...]
    o_ref[...] = x_ref[...] + (route[:, 2:3] * buf[0] + route[:, 3:4] * buf[1])


def _combine(eo, dst, x, route, rows):
    t, d = x.shape
    steps = t // rows
    return pl.pallas_call(
        functools.partial(_combine_kernel, rows=rows),
        out_shape=jax.ShapeDtypeStruct((t, d), F32),
        grid=(steps,),
        in_specs=[pl.BlockSpec((1, 1, 2 * rows), lambda i: (i, 0, 0), memory_space=pltpu.SMEM),
                  pl.BlockSpec(memory_space=pl.ANY),
                  pl.BlockSpec((rows, d), lambda i: (i, 0)),
                  pl.BlockSpec((rows, LANES), lambda i: (i, 0))],
        out_specs=pl.BlockSpec((rows, d), lambda i: (i, 0)),
        scratch_shapes=[pltpu.VMEM((2, rows, d), F32), pltpu.SemaphoreType.DMA((1,))],
        compiler_params=_params("arbitrary"),
        name="moe_combine",
    )(dst, eo, x, route)


def _moe_ffn(x, g, w_router, w1, w3, layer, w2):
    t, d = x.shape
    n_exp = w_router.shape[1]
    tm = EXPERT_ROW_TILE
    rows_g = _tile(tm, GATHER_ROWS, 8)
    rows_c = _tile(t, GATHER_ROWS, 8)
    n_tiles = -(-(TOP_K * t) // tm) + n_exp
    r = n_tiles * tm

    h, route = _router(x, g, w_router)

    e_flat = jnp.concatenate([route[:, 0], route[:, 1]]).astype(jnp.int32)
    onehot = (e_flat[:, None] == jnp.arange(n_exp, dtype=jnp.int32)[None, :]).astype(jnp.int32)
    rank = jnp.sum((jnp.cumsum(onehot, axis=0) - onehot) * onehot, axis=1)
    counts = jnp.sum(onehot, axis=0)
    tiles_per = (counts + tm - 1) // tm
    tile_end = jnp.cumsum(tiles_per)
    tile_start = tile_end - tiles_per
    n_active = tile_end[-1]
    dst = tile_start[e_flat] * tm + rank
    tile_ids = jnp.arange(n_tiles, dtype=jnp.int32)
    tile_expert = jnp.minimum(jnp.searchsorted(tile_end, tile_ids, side="right"), n_exp - 1)
    last_expert = tile_expert[jnp.maximum(n_active - 1, 0)]
    tile_rows = jnp.clip(counts[tile_expert] - (tile_ids - tile_start[tile_expert]) * tm, 0, tm)
    tile_rows = jnp.where(tile_ids < n_active, tile_rows, 0).astype(jnp.int32)
    tile_expert = jnp.where(tile_ids < n_active, tile_expert, last_expert).astype(jnp.int32)
    order = jnp.argsort(e_flat, stable=True).astype(jnp.int32)
    group_off = jnp.cumsum(counts) - counts
    slot = jnp.arange(r, dtype=jnp.int32)
    slot_e = tile_expert[slot // tm]
    within = slot - tile_start[slot_e] * tm
    valid = (within < counts[slot_e]) & (slot // tm < n_active)
    src = jnp.where(valid, order[jnp.clip(group_off[slot_e] + within, 0, TOP_K * t - 1)] % t, 0)
    src = src.astype(jnp.int32)

    na = n_active.astype(jnp.int32).reshape(1)
    xg = _gather_rows(h, src, na * (tm // rows_g), rows_g)
    gu = _grouped_gateup(xg, w1, w3, layer, tile_expert, na, tile_rows, tm)
    eo = _grouped_down(gu, w2, tile_expert, na, tile_rows, tm)
    steps = t // rows_c
    dst2 = jnp.concatenate([dst[:t].reshape(steps, 1, rows_c), dst[t:].reshape(steps, 1, rows_c)],
                           axis=2).astype(jnp.int32)
    return _combine(eo, dst2, x, route, rows_c)


def kernel(x_prompt, x_sample, cache_k, cache_v, cache_conv, w_in, w_out, conv_w, attn_norm,
           q_norm, k_norm, lambda_q1, lambda_k1, lambda_q2, lambda_k2, subln, ffn_norm,
           w1_dense, w3_dense, w2_dense, w_router, w1_exp, w3_exp, w2_exp):
    depth = w_in.shape[0]
    nbp, seq, d = x_prompt.shape
    nbs, sseq, _ = x_sample.shape
    past, heads, _, dk = cache_k.shape[2:]
    dv = cache_v.shape[-1]
    cdim = cache_conv.shape[-1]
    hw = 2 * dk
    assert dv == hw == LANES, "one head (both maps / the value row) must span one lane tile"
    qk_dim = heads * hw
    attn_dim = heads * dv
    tp, ts = nbp * seq, nbs * sseq
    assert tp % sseq == 0 and seq % CHUNK == 0
    c_q, c_k, c_v = 0, qk_dim, 2 * qk_dim
    c_b = c_v + attn_dim
    c_c, c_h = c_b + cdim, c_b + 2 * cdim

    x = jnp.concatenate([x_prompt.reshape(tp, d), x_sample.reshape(ts, d)], axis=0)
    zero_state = jnp.zeros((nbp, 2, cdim), F32)
    ck = cache_k.reshape(depth, nbs, past, qk_dim)
    cv = cache_v.reshape(depth, nbs, past, attn_dim)

    ks, vs, convs_p, convs_s = [], [], [], []
    for l in range(depth):
        lam0 = _lambda_init(l)
        lam_rows = [a[l].reshape(1, dk) for a in (lambda_q1, lambda_k1, lambda_q2, lambda_k2)]
        sub = subln[l].reshape(1, dv)

        h = _rmsnorm(x, attn_norm[l])
        p = _matmul(h, w_in[l].astype(BF16), F32)
        q = _groupnorm(p, c_q, qk_dim, q_norm[l], dk, dk ** -0.5, BF16)
        k = _groupnorm(p, c_k, qk_dim, k_norm[l], dk, 1.0, F32)
        o_p = _attn_prompt(q, k, 0, p, c_v // hw, nbp, seq, heads, dk, lam_rows, sub, lam0)
        o_s = _attn_sample(q, k, p, c_v // hw, ck, cv, l, tp, nbs, sseq, heads, dk, lam_rows,
                           sub, lam0)
        y_p, nc_p = _conv(p, c_b, c_c, c_h, cdim, 0, nbp, seq, zero_state, conv_w[l])
        y_s, nc_s = _conv(p, c_b, c_c, c_h, cdim, tp, nbs, sseq, cache_conv[l], conv_w[l])
        x = _outproj(jnp.concatenate([o_p, o_s], axis=0), jnp.concatenate([y_p, y_s], axis=0),
                     w_out[l].astype(BF16), x)
        ks.append(k)
        vs.append(p[:, c_v:c_v + attn_dim])
        convs_p.append(nc_p)
        convs_s.append(nc_s)

        m = l // 2
        if l % 2 == 0:
            hf = _rmsnorm(x, ffn_norm[l])
            gu = _gateup(hf, w1_dense[m].astype(BF16), w3_dense[m].astype(BF16))
            x = _down(gu, w2_dense[m].astype(BF16), x)
        else:
            x = _moe_ffn(x, ffn_norm[l], w_router[m], w1_exp, w3_exp, m, w2_exp[m].astype(BF16))

    k_all = jnp.stack(ks)
    v_all = jnp.stack(vs)
    return (x[:tp].reshape(nbp, seq, d),
            x[tp:].reshape(nbs, sseq, d),
            k_all[:, :tp].reshape(depth, nbp, seq, heads, 2, dk),
            v_all[:, :tp].reshape(depth, nbp, seq, heads, dv),
            jnp.stack(convs_p),
            k_all[:, tp:].reshape(depth, nbs, sseq, heads, 2, dk),
            v_all[:, tp:].reshape(depth, nbs, sseq, heads, dv),
            jnp.stack(convs_s))
```

```python
import functools
import math

import jax
import jax.numpy as jnp
from jax import lax
from jax.experimental import pallas as pl
from jax.experimental.pallas import tpu as pltpu

CHUNK = 64
TOP_K = 2
EPS = 1e-6
NEG_INF = -1e30
F32 = jnp.float32
BF16 = jnp.bfloat16

LANES = 128
MXU_DIM = 256
VMEM_LIMIT = 56 * 1024 * 1024

ROW_TILE = 1100
EXPERT_ROW_TILE = 1152
EXPERT_SUB_ROWS = 384
GATHER_ROWS = 256
ATTN_BLOCK = 512


def _lambda_init(l):
    return 0.8 - 0.6 * math.exp(-0.3 * l)


def _tile(n, pref, mult):
    if n <= pref:
        return n
    best = None
    for t in range(mult, pref + 1, mult):
        if n % t == 0:
            best = t
    assert best is not None, (n, pref, mult)
    return best


def _params(*sem):
    return pltpu.CompilerParams(dimension_semantics=sem, vmem_limit_bytes=VMEM_LIMIT)


def _dot(a, b):
    return jnp.dot(a, b, preferred_element_type=F32)


def _rmsnorm_kernel(x_ref, g_ref, o_ref):
    x = x_ref[...]
    y = x * lax.rsqrt(jnp.mean(x * x, axis=-1, keepdims=True) + EPS)
    o_ref[...] = (y * g_ref[...]).astype(o_ref.dtype)


def _rmsnorm(x, g):
    t, d = x.shape
    tr = _tile(t, 256, 16)
    return pl.pallas_call(
        _rmsnorm_kernel,
        out_shape=jax.ShapeDtypeStruct((t, d), BF16),
        grid=(t // tr,),
        in_specs=[pl.BlockSpec((tr, d), lambda i: (i, 0)),
                  pl.BlockSpec((1, d), lambda i: (0, 0))],
        out_specs=pl.BlockSpec((tr, d), lambda i: (i, 0)),
        compiler_params=_params("parallel"),
        name="rmsnorm",
    )(x, g.reshape(1, d))


def _mm_kernel(a_ref, b_ref, o_ref):
    o_ref[...] = _dot(a_ref[...], b_ref[...]).astype(o_ref.dtype)


def _matmul(a, b, col_off, width, out_dtype, tn_pref=1024):
    m, k = a.shape
    tm = _tile(m, ROW_TILE, 16)
    tn = _tile(math.gcd(width, col_off) if col_off else width, tn_pref, LANES)
    off = col_off // tn
    return pl.pallas_call(
        _mm_kernel,
        out_shape=jax.ShapeDtypeStruct((m, width), out_dtype),
        grid=(m // tm, width // tn),
        in_specs=[pl.BlockSpec((tm, k), lambda i, j: (i, 0)),
                  pl.BlockSpec((k, tn), lambda i, j: (0, j + off))],
        out_specs=pl.BlockSpec((tm, tn), lambda i, j: (i, j)),
        compiler_params=_params("parallel", "parallel"),
        name="in_proj",
    )(a, b)


def _proj_norm_kernel(a_ref, b_ref, g_ref, gm_ref, o_ref, *, group, scale):
    acc = _dot(a_ref[...], b_ref[...])
    gm = gm_ref[...]
    w = gm.shape[0]
    for c in range(0, acc.shape[1], w):
        x = acc[:, c:c + w]
        sq = x * x
        hi = sq.astype(BF16)
        lo = (sq - hi.astype(F32)).astype(BF16)
        ssq = _dot(hi, gm) + _dot(lo, gm)
        y = x * lax.rsqrt(ssq * (1.0 / group) + EPS)
        o_ref[:, c:c + w] = ((y * g_ref[...]) * scale).astype(o_ref.dtype)


def _proj_norm(a, b, col_off, width, gain, group, scale, out_dtype):
    m, k = a.shape
    gw = MXU_DIM
    tm = _tile(m, ROW_TILE, 16)
    tn = _tile(math.gcd(width, col_off) if col_off else width, 512, gw)
    assert gw % group == 0
    ids = jnp.arange(gw) // group
    gm = (ids[:, None] == ids[None, :]).astype(BF16)
    g = jnp.tile(gain.astype(F32), gw // group).reshape(1, gw)
    off = col_off // tn
    return pl.pallas_call(
        functools.partial(_proj_norm_kernel, group=group, scale=scale),
        out_shape=jax.ShapeDtypeStruct((m, width), out_dtype),
        grid=(m // tm, width // tn),
        in_specs=[pl.BlockSpec((tm, k), lambda i, j: (i, 0)),
                  pl.BlockSpec((k, tn), lambda i, j: (0, j + off)),
                  pl.BlockSpec((1, gw), lambda i, j: (0, 0)),
                  pl.BlockSpec((gw, gw), lambda i, j: (0, 0))],
        out_specs=pl.BlockSpec((tm, tn), lambda i, j: (i, j)),
        compiler_params=_params("parallel", "parallel"),
        name="qk_proj_norm",
    )(a, b, g, gm)


def _diff_lambda(lq1, lk1, lq2, lk2, lam_init):
    a = jnp.exp(jnp.sum(lq1[...] * lk1[...], axis=-1, keepdims=True))
    b = jnp.exp(jnp.sum(lq2[...] * lk2[...], axis=-1, keepdims=True))
    return a - b + lam_init


def _finish_heads(a1, l1, a2, l2, lam, sub, lam_init):
    o = a1 / l1 - lam * (a2 / l2)
    y = o * lax.rsqrt(jnp.mean(o * o, axis=-1, keepdims=True) + EPS)
    return (y * sub) * (1.0 - lam_init)


def _split_maps(q, dk):
    lane = lax.broadcasted_iota(jnp.int32, q.shape, 1)
    zero = jnp.zeros_like(q)
    return jnp.where(lane < dk, q, zero), jnp.where(lane >= dk, q, zero)


def _qk(q, k):
    return lax.dot_general(q, k, (((1,), (1,)), ((), ())), preferred_element_type=F32)


def _attn_prompt_kernel(q_ref, k_ref, v_ref, lq1, lk1, lq2, lk2, sub_ref, o_ref,
                        kb, vt, qt, m_s, l_s, acc, s_ref, p_ref, *, tq, dk, lam_init):
    qi = pl.program_id(2)
    nq = kb.shape[0]

    @pl.when(qi == 0)
    def _():
        for jj in range(nq):
            kb[jj] = k_ref[jj * tq:(jj + 1) * tq, :].astype(BF16)
            vt[jj] = v_ref[jj * tq:(jj + 1) * tq, :].T.astype(BF16)

    q = q_ref[...].astype(F32)
    lane = lax.broadcasted_iota(jnp.int32, q.shape, 1)
    qt[:, 0:tq] = jnp.where(lane < dk, q, 0.0).T.astype(BF16)
    qt[:, tq:2 * tq] = jnp.where(lane >= dk, q, 0.0).T.astype(BF16)
    m_s[...] = jnp.full(m_s.shape, NEG_INF, F32)
    l_s[...] = jnp.zeros(l_s.shape, F32)
    acc[...] = jnp.zeros(acc.shape, F32)

    def block(j, masked):
        kblk = kb[j]
        vblk = vt[j]
        for lo in (0, tq):
            s_ref[:, lo:lo + tq] = _dot(kblk, qt[:, lo:lo + tq])
        for lo in (0, tq):
            for c in range(lo, lo + tq, LANES):
                s = s_ref[:, c:c + LANES]
                if masked:
                    key = lax.broadcasted_iota(jnp.int32, s.shape, 0)
                    qry = lax.broadcasted_iota(jnp.int32, s.shape, 1) + (c - lo)
                    s = jnp.where(key // CHUNK <= qry // CHUNK, s, NEG_INF)
                m_old = m_s[:, c:c + LANES]
                m_new = jnp.maximum(m_old, jnp.max(s, axis=0, keepdims=True))
                alpha = jnp.exp(m_old - m_new)
                p = jnp.exp(s - m_new)
                l_s[:, c:c + LANES] = (alpha * l_s[:, c:c + LANES]
                                       + jnp.sum(p, axis=0, keepdims=True))
                m_s[:, c:c + LANES] = m_new
                p_ref[:, c:c + LANES] = p.astype(BF16)
                acc[:, c:c + LANES] = alpha * acc[:, c:c + LANES]
            acc[:, lo:lo + tq] += _dot(vblk, p_ref[:, lo:lo + tq])

    def body(j, c):
        block(j, False)
        return c

    lax.fori_loop(0, qi, body, 0)
    block(qi, True)
    lam = _diff_lambda(lq1, lk1, lq2, lk2, lam_init)
    r = acc[...] / l_s[...]
    o = (r[:, 0:tq] - lam * r[:, tq:2 * tq]).T
    y = o * lax.rsqrt(jnp.mean(o * o, axis=-1, keepdims=True) + EPS)
    o_ref[...] = ((y * sub_ref[...]) * (1.0 - lam_init)).astype(o_ref.dtype)


def _attn_prompt(q, kv_src, k_col, v_src, v_col, nb, seq, heads, dk, lam_rows, sub, lam_init):
    hw = 2 * dk
    tq = _tile(seq, ATTN_BLOCK, LANES)
    nq = seq // tq
    vec = lambda i, h, j: (0, 0)
    return pl.pallas_call(
        functools.partial(_attn_prompt_kernel, tq=tq, dk=dk, lam_init=lam_init),
        out_shape=jax.ShapeDtypeStruct((nb * seq, heads * hw), BF16),
        grid=(nb, heads, nq),
        in_specs=[pl.BlockSpec((tq, hw), lambda b, h, j: (b * nq + j, h)),
                  pl.BlockSpec((seq, hw), lambda b, h, j: (b, k_col + h)),
                  pl.BlockSpec((seq, hw), lambda b, h, j: (b, v_col + h)),
                  pl.BlockSpec((1, dk), vec), pl.BlockSpec((1, dk), vec),
                  pl.BlockSpec((1, dk), vec), pl.BlockSpec((1, dk), vec),
                  pl.BlockSpec((1, hw), vec)],
        out_specs=pl.BlockSpec((tq, hw), lambda b, h, j: (b * nq + j, h)),
        scratch_shapes=[pltpu.VMEM((nq, tq, hw), BF16), pltpu.VMEM((nq, hw, tq), BF16),
                        pltpu.VMEM((hw, 2 * tq), BF16), pltpu.VMEM((1, 2 * tq), F32),
                        pltpu.VMEM((1, 2 * tq), F32), pltpu.VMEM((hw, 2 * tq), F32),
                        pltpu.VMEM((tq, 2 * tq), F32), pltpu.VMEM((tq, 2 * tq), BF16)],
        compiler_params=_params("parallel", "parallel", "arbitrary"),
        name="attn_prompt",
    )(q, kv_src, v_src, *lam_rows, sub)


def _attn_sample_kernel(q_ref, kn_ref, vn_ref, kp_ref, vp_ref, lq1, lk1, lq2, lk2, sub_ref, o_ref,
                        *, dk, lam_init):
    hw = 2 * dk
    lam = _diff_lambda(lq1, lk1, lq2, lk2, lam_init)
    for c in range(0, q_ref.shape[1], hw):
        q1, q2 = _split_maps(q_ref[:, c:c + hw], dk)
        kp = kp_ref[:, c:c + hw].astype(BF16)
        vp = vp_ref[:, c:c + hw].astype(BF16)
        kn = kn_ref[:, c:c + hw].astype(BF16)
        vn = vn_ref[:, c:c + hw].astype(BF16)

        def one_map(q):
            sp = _qk(q, kp)
            sn = _qk(q, kn)
            m = jnp.maximum(jnp.max(sp, axis=-1, keepdims=True),
                            jnp.max(sn, axis=-1, keepdims=True))
            ep = jnp.exp(sp - m)
            en = jnp.exp(sn - m)
            l = jnp.sum(ep, axis=-1, keepdims=True) + jnp.sum(en, axis=-1, keepdims=True)
            a = _dot(ep.astype(BF16), vp) + _dot(en.astype(BF16), vn)
            return l, a

        l1, a1 = one_map(q1)
        l2, a2 = one_map(q2)
        o_ref[:, c:c + hw] = _finish_heads(a1, l1, a2, l2, lam, sub_ref[...],
                                           lam_init).astype(o_ref.dtype)


def _attn_sample(q, k_new, v_src, v_col, past_k, past_v, layer, row_off, nb, seq, heads, dk,
                 lam_rows, sub, lam_init):
    hw = 2 * dk
    past = past_k.shape[2]
    rb = row_off // seq
    hb = next(n for n in (4, 2, 1) if heads % n == 0)
    bw = hb * hw
    assert (v_col * hw) % bw == 0
    vc = v_col * hw // bw
    vec = lambda b, h: (0, 0)
    return pl.pallas_call(
        functools.partial(_attn_sample_kernel, dk=dk, lam_init=lam_init),
        out_shape=jax.ShapeDtypeStruct((nb * seq, heads * hw), BF16),
        grid=(nb, heads // hb),
        in_specs=[pl.BlockSpec((seq, bw), lambda b, h: (rb + b, h)),
                  pl.BlockSpec((seq, bw), lambda b, h: (rb + b, h)),
                  pl.BlockSpec((seq, bw), lambda b, h: (rb + b, vc + h)),
                  pl.BlockSpec((None, None, past, bw), lambda b, h: (layer, b, 0, h)),
                  pl.BlockSpec((None, None, past, bw), lambda b, h: (layer, b, 0, h)),
                  pl.BlockSpec((1, dk), vec), pl.BlockSpec((1, dk), vec),
                  pl.BlockSpec((1, dk), vec), pl.BlockSpec((1, dk), vec),
                  pl.BlockSpec((1, hw), vec)],
        out_specs=pl.BlockSpec((seq, bw), lambda b, h: (b, h)),
        compiler_params=_params("parallel", "parallel"),
        name="attn_sample",
    )(q, k_new, v_src, past_k, past_v, *lam_rows, sub)


def _conv_kernel(bg_ref, cg_ref, hc_ref, st_ref, cw_ref, y_ref, nc_ref, carry, *, ts):
    s = pl.program_id(2)
    u = cg_ref[...] * hc_ref[...]

    @pl.when(s == 0)
    def _():
        carry[6:8, :] = st_ref[...]

    prev = carry[...]
    p2 = prev[6:7, :]
    p1 = prev[7:8, :]
    row = lax.broadcasted_iota(jnp.int32, u.shape, 0)
    u1 = jnp.where(row == 0, p1, pltpu.roll(u, 1, 0))
    u2 = jnp.where(row == 0, p2, jnp.where(row == 1, p1, pltpu.roll(u, 2, 0)))
    cw = cw_ref[...]
    y = bg_ref[...] * (cw[0:1, :] * u2 + cw[1:2, :] * u1 + cw[2:3, :] * u)
    y_ref[...] = y.astype(y_ref.dtype)
    carry[...] = u[ts - 8:ts, :]

    @pl.when(s == pl.num_programs(2) - 1)
    def _():
        nc_ref[...] = u[ts - 2:ts, :]


def _conv(p, col_b, col_c, col_h, width, row_off, nb, seq, state, cw):
    assert seq >= 8 and state.shape[1] == 2 and cw.shape[0] == 3
    ts = _tile(seq, 512, 8)
    tc = _tile(width, 512, LANES)
    ns = seq // ts
    rb = row_off // ts
    ob, oc, oh = col_b // tc, col_c // tc, col_h // tc
    blk = lambda o: pl.BlockSpec((ts, tc), lambda b, c, s: (rb + b * ns + s, o + c))
    return pl.pallas_call(
        functools.partial(_conv_kernel, ts=ts),
        out_shape=(jax.ShapeDtypeStruct((nb * seq, width), BF16),
                   jax.ShapeDtypeStruct((nb, 2, width), F32)),
        grid=(nb, width // tc, ns),
        in_specs=[blk(ob), blk(oc), blk(oh),
                  pl.BlockSpec((None, 2, tc), lambda b, c, s: (b, 0, c)),
                  pl.BlockSpec((3, tc), lambda b, c, s: (0, c))],
        out_specs=(pl.BlockSpec((ts, tc), lambda b, c, s: (b * ns + s, c)),
                   pl.BlockSpec((None, 2, tc), lambda b, c, s: (b, 0, c))),
        scratch_shapes=[pltpu.VMEM((8, tc), F32)],
        compiler_params=_params("parallel", "parallel", "arbitrary"),
        name="short_conv",
    )(p, p, p, state, cw)


def _outproj_kernel(a1_ref, a2_ref, b1_ref, b2_ref, x_ref, o_ref):
    o_ref[...] = x_ref[...] + (_dot(a1_ref[...], b1_ref[...]) + _dot(a2_ref[...], b2_ref[...]))


def _outproj(a1, a2, w, x):
    m, k1 = a1.shape
    k2 = a2.shape[1]
    assert k1 == k2 and w.shape[0] == k1 + k2
    n = w.shape[1]
    tm = _tile(m, ROW_TILE, 16)
    tn = _tile(n, 512, LANES)
    return pl.pallas_call(
        _outproj_kernel,
        out_shape=jax.ShapeDtypeStruct((m, n), F32),
        grid=(m // tm, n // tn),
        in_specs=[pl.BlockSpec((tm, k1), lambda i, j: (i, 0)),
                  pl.BlockSpec((tm, k2), lambda i, j: (i, 0)),
                  pl.BlockSpec((k1, tn), lambda i, j: (0, j)),
                  pl.BlockSpec((k2, tn), lambda i, j: (1, j)),
                  pl.BlockSpec((tm, tn), lambda i, j: (i, j))],
        out_specs=pl.BlockSpec((tm, tn), lambda i, j: (i, j)),
        compiler_params=_params("parallel", "parallel"),
        name="out_proj",
    )(a1, a2, w, w, x)


def _silu_mul(g, u):
    return (g / (1.0 + jnp.exp(-g))) * u


def _gateup_kernel(a_ref, w1_ref, w3_ref, o_ref):
    a = a_ref[...]
    o_ref[...] = _silu_mul(_dot(a, w1_ref[...]), _dot(a, w3_ref[...])).astype(o_ref.dtype)


def _gateup(a, w1, w3):
    m, k = a.shape
    n = w1.shape[1]
    tm = _tile(m, ROW_TILE, 16)
    tn = _tile(n, 512, LANES)
    return pl.pallas_call(
        _gateup_kernel,
        out_shape=jax.ShapeDtypeStruct((m, n), BF16),
        grid=(m // tm, n // tn),
        in_specs=[pl.BlockSpec((tm, k), lambda i, j: (i, 0)),
                  pl.BlockSpec((k, tn), lambda i, j: (0, j)),
                  pl.BlockSpec((k, tn), lambda i, j: (0, j))],
        out_specs=pl.BlockSpec((tm, tn), lambda i, j: (i, j)),
        compiler_params=_params("parallel", "parallel"),
        name="ffn_gate_up",
    )(a, w1, w3)


def _down_kernel(a_ref, b_ref, x_ref, o_ref):
    d = _dot(a_ref[...], b_ref[...])

    @pl.when(pl.program_id(2) == 0)
    def _():
        o_ref[...] = x_ref[...] + d

    @pl.when(pl.program_id(2) > 0)
    def _():
        o_ref[...] += d


def _down(a, w, x):
    m, k = a.shape
    n = w.shape[1]
    tm = _tile(m, ROW_TILE, 16)
    tn = _tile(n, 1024, LANES)
    tk = _tile(k, 2048, MXU_DIM)
    return pl.pallas_call(
        _down_kernel,
        out_shape=jax.ShapeDtypeStruct((m, n), F32),
        grid=(m // tm, n // tn, k // tk),
        in_specs=[pl.BlockSpec((tm, tk), lambda i, j, kk: (i, kk)),
                  pl.BlockSpec((tk, tn), lambda i, j, kk: (kk, j)),
                  pl.BlockSpec((tm, tn), lambda i, j, kk: (i, j))],
        out_specs=pl.BlockSpec((tm, tn), lambda i, j, kk: (i, j)),
        compiler_params=_params("parallel", "parallel", "arbitrary"),
        name="ffn_down",
    )(a, w, x)


def _grouped_gateup_kernel(te_ref, na_ref, rows_ref, a_ref, w1_ref, w3_ref, o_ref, *, sub):
    rows = rows_ref[pl.program_id(0)]
    tm = a_ref.shape[0]

    def compute(lo, size):
        a = a_ref[lo:lo + size, :]
        g = _dot(a, w1_ref[...].astype(BF16))
        u = _dot(a, w3_ref[...].astype(BF16))
        o_ref[lo:lo + size, :] = _silu_mul(g, u).astype(o_ref.dtype)

    @pl.when(rows == tm)
    def _():
        compute(0, tm)

    for s in range(tm // sub):
        @pl.when((rows < tm) & (s * sub < rows))
        def _():
            compute(s * sub, sub)

        @pl.when((rows < tm) & (s * sub >= rows))
        def _():
            o_ref[s * sub:(s + 1) * sub, :] = jnp.zeros((sub, o_ref.shape[1]), o_ref.dtype)


def _grouped_gateup(xg, w1, w3, layer, tile_expert, n_active, tile_rows, tm):
    r, k = xg.shape
    n = w1.shape[3]
    tn = _tile(n, 256, LANES)
    nj = n // tn
    sub = _tile(tm, EXPERT_SUB_ROWS, 16)
    wspec = pl.BlockSpec(
        (None, None, k, tn),
        lambda i, j, te, na, tr: (layer, te[i], 0, jnp.where(i < na[0], j, nj - 1)))
    return pl.pallas_call(
        functools.partial(_grouped_gateup_kernel, sub=sub),
        out_shape=jax.ShapeDtypeStruct((r, n), BF16),
        grid_spec=pltpu.PrefetchScalarGridSpec(
            num_scalar_prefetch=3,
            grid=(r // tm, nj),
            in_specs=[pl.BlockSpec((tm, k),
                                   lambda i, j, te, na, tr: (jnp.minimum(i, na[0] - 1), 0)),
                      wspec, wspec],
            out_specs=pl.BlockSpec((tm, tn), lambda i, j, te, na, tr: (i, j))),
        compiler_params=_params("parallel", "parallel"),
        name="moe_gate_up",
    )(tile_expert, n_active, tile_rows, xg, w1, w3)


def _grouped_down_kernel(te_ref, na_ref, rows_ref, a_ref, b_ref, o_ref, *, sub):
    rows = rows_ref[pl.program_id(0)]
    first = pl.program_id(2) == 0
    tm = a_ref.shape[0]

    def compute(lo, size):
        d = _dot(a_ref[lo:lo + size, :], b_ref[...])

        @pl.when(first)
        def _():
            o_ref[lo:lo + size, :] = d

        @pl.when(jnp.logical_not(first))
        def _():
            o_ref[lo:lo + size, :] += d

    @pl.when(rows == tm)
    def _():
        compute(0, tm)

    for s in range(tm // sub):
        @pl.when((rows < tm) & (s * sub < rows))
        def _():
            compute(s * sub, sub)

        @pl.when((rows < tm) & (s * sub >= rows) & first)
        def _():
            o_ref[s * sub:(s + 1) * sub, :] = jnp.zeros((sub, o_ref.shape[1]), o_ref.dtype)


def _grouped_down(g, w, tile_expert, n_active, tile_rows, tm):
    r, k = g.shape
    n = w.shape[2]
    tn = _tile(n, 1024, LANES)
    tk = _tile(k, 2048, MXU_DIM)
    nj, nk = n // tn, k // tk
    sub = _tile(tm, EXPERT_SUB_ROWS, 16)

    def a_map(i, j, kk, te, na, tr):
        act = i < na[0]
        return jnp.minimum(i, na[0] - 1), jnp.where(act, kk, nk - 1)

    def b_map(i, j, kk, te, na, tr):
        act = i < na[0]
        return te[i], jnp.where(act, kk, nk - 1), jnp.where(act, j, nj - 1)

    return pl.pallas_call(
        functools.partial(_grouped_down_kernel, sub=sub),
        out_shape=jax.ShapeDtypeStruct((r, n), F32),
        grid_spec=pltpu.PrefetchScalarGridSpec(
            num_scalar_prefetch=3,
            grid=(r // tm, nj, nk),
            in_specs=[pl.BlockSpec((tm, tk), a_map),
                      pl.BlockSpec((None, tk, tn), b_map)],
            out_specs=pl.BlockSpec((tm, tn), lambda i, j, kk, te, na, tr: (i, j))),
        compiler_params=_params("parallel", "parallel", "arbitrary"),
        name="moe_down",
    )(tile_expert, n_active, tile_rows, g, w)


def _router_kernel(x_ref, g_ref, wr_ref, h_ref, r_ref, *, n_exp):
    x = x_ref[...]
    h = (x * lax.rsqrt(jnp.mean(x * x, axis=-1, keepdims=True) + EPS)) * g_ref[...]
    h_ref[...] = h
    logits = jnp.dot(h, wr_ref[...], precision=lax.Precision.HIGHEST, preferred_element_type=F32)
    lane = lax.broadcasted_iota(jnp.int32, logits.shape, 1)
    lanef = lane.astype(F32)
    lg = jnp.where(lane < n_exp, logits, -jnp.inf)
    m1 = jnp.max(lg, axis=-1, keepdims=True)
    i1 = jnp.min(jnp.where(lg == m1, lanef, float(LANES)), axis=-1, keepdims=True)
    lg2 = jnp.where(lanef == i1, -jnp.inf, lg)
    m2 = jnp.max(lg2, axis=-1, keepdims=True)
    i2 = jnp.min(jnp.where(lg2 == m2, lanef, float(LANES)), axis=-1, keepdims=True)
    e = jnp.exp(m2 - m1)
    den = 1.0 + e
    r_ref[...] = jnp.where(lane == 0, i1,
                 jnp.where(lane == 1, i2,
                 jnp.where(lane == 2, 1.0 / den,
                 jnp.where(lane == 3, e / den, 0.0))))


def _router(x, g, w_router):
    t, d = x.shape
    n_exp = w_router.shape[1]
    assert n_exp <= LANES and TOP_K == 2
    wr = jnp.zeros((d, LANES), F32).at[:, :n_exp].set(w_router.astype(F32))
    tr = _tile(t, 256, 8)
    return pl.pallas_call(
        functools.partial(_router_kernel, n_exp=n_exp),
        out_shape=(jax.ShapeDtypeStruct((t, d), F32), jax.ShapeDtypeStruct((t, LANES), F32)),
        grid=(t // tr,),
        in_specs=[pl.BlockSpec((tr, d), lambda i: (i, 0)),
                  pl.BlockSpec((1, d), lambda i: (0, 0)),
                  pl.BlockSpec((d, LANES), lambda i: (0, 0))],
        out_specs=(pl.BlockSpec((tr, d), lambda i: (i, 0)),
                   pl.BlockSpec((tr, LANES), lambda i: (i, 0))),
        compiler_params=_params("parallel"),
        name="ffn_norm_router",
    )(x, g.reshape(1, d), wr)


def _row_copy(src_hbm, idx, dst_vmem, row, sem):
    return pltpu.make_async_copy(src_hbm.at[pl.ds(idx, 1), :], dst_vmem.at[pl.ds(row, 1), :], sem)


def _gather_kernel(na_ref, src_ref, nxt_ref, h_hbm, o_ref, buf, sem, *, rows):
    i = pl.program_id(0)
    slot = i % 2

    def issue(idx_ref, sl):
        def body(r, c):
            _row_copy(h_hbm, idx_ref[0, 0, r], buf.at[sl], r, sem.at[sl]).start()
            return c
        lax.fori_loop(0, rows, body, 0, unroll=8)

    @pl.when(i == 0)
    def _():
        issue(src_ref, 0)

    @pl.when(i + 1 < na_ref[0])
    def _():
        issue(nxt_ref, 1 - slot)

    @pl.when(i < na_ref[0])
    def _():
        def wait(r, c):
            _row_copy(h_hbm, src_ref[0, 0, r], buf.at[slot], r, sem.at[slot]).wait()
            return c
        lax.fori_loop(0, rows, wait, 0, unroll=8)
        o_ref[...] = buf[slot].astype(o_ref.dtype)

    @pl.when(i >= na_ref[0])
    def _():
        o_ref[...] = jnp.zeros_like(o_ref)


def _gather_rows(h, src, n_active_steps, rows):
    r = src.shape[0]
    d = h.shape[1]
    steps = r // rows
    src3 = src.reshape(steps, 1, rows)
    idx_spec = lambda f: pl.BlockSpec((1, 1, rows), f, memory_space=pltpu.SMEM)
    return pl.pallas_call(
        functools.partial(_gather_kernel, rows=rows),
        out_shape=jax.ShapeDtypeStruct((r, d), BF16),
        grid_spec=pltpu.PrefetchScalarGridSpec(
            num_scalar_prefetch=1,
            grid=(steps,),
            in_specs=[idx_spec(lambda i, na: (i, 0, 0)),
                      idx_spec(lambda i, na: (jnp.minimum(i + 1, steps - 1), 0, 0)),
                      pl.BlockSpec(memory_space=pl.ANY)],
            out_specs=pl.BlockSpec((rows, d), lambda i, na: (i, 0)),
            scratch_shapes=[pltpu.VMEM((2, rows, d), F32), pltpu.SemaphoreType.DMA((2,))]),
        compiler_params=_params("arbitrary"),
        name="moe_gather",
    )(n_active_steps, src3, src3, h)


def _combine_kernel(dst_ref, eo_hbm, x_ref, r_ref, o_ref, buf, sem, *, rows):
    def start(r, c):
        _row_copy(eo_hbm, dst_ref[0, 0, r], buf.at[0], r, sem.at[0]).start()
        _row_copy(eo_hbm, dst_ref[0, 0, rows + r], buf.at[1], r, sem.at[0]).start()
        return c

    def wait(r, c):
        _row_copy(eo_hbm, dst_ref[0, 0, r], buf.at[0], r, sem.at[0]).wait()
        _row_copy(eo_hbm, dst_ref[0, 0, rows + r], buf.at[1], r, sem.at[0]).wait()
        return c

    lax.fori_loop(0, rows, start, 0, unroll=8)
    lax.fori_loop(0, rows, wait, 0, unroll=8)
    route = r_ref[...]
    o_ref[...] = x_ref[...] + (route[:, 2:3] * buf[0] + route[:, 3:4] * buf[1])


def _combine(eo, dst, x, route, rows):
    t, d = x.shape
    steps = t // rows
    return pl.pallas_call(
        functools.partial(_combine_kernel, rows=rows),
        out_shape=jax.ShapeDtypeStruct((t, d), F32),
        grid=(steps,),
        in_specs=[pl.BlockSpec((1, 1, 2 * rows), lambda i: (i, 0, 0), memory_space=pltpu.SMEM),
                  pl.BlockSpec(memory_space=pl.ANY),
                  pl.BlockSpec((rows, d), lambda i: (i, 0)),
                  pl.BlockSpec((rows, LANES), lambda i: (i, 0))],
        out_specs=pl.BlockSpec((rows, d), lambda i: (i, 0)),
        scratch_shapes=[pltpu.VMEM((2, rows, d), F32), pltpu.SemaphoreType.DMA((1,))],
        compiler_params=_params("arbitrary"),
        name="moe_combine",
    )(dst, eo, x, route)


def _moe_ffn(x, g, w_router, w1, w3, layer, w2):
    t, d = x.shape
    n_exp = w_router.shape[1]
    tm = EXPERT_ROW_TILE
    rows_g = _tile(tm, GATHER_ROWS, 8)
    rows_c = _tile(t, GATHER_ROWS, 8)
    n_tiles = -(-(TOP_K * t) // tm) + n_exp
    r = n_tiles * tm

    h, route = _router(x, g, w_router)

    e_flat = jnp.concatenate([route[:, 0], route[:, 1]]).astype(jnp.int32)
    onehot = (e_flat[:, None] == jnp.arange(n_exp, dtype=jnp.int32)[None, :]).astype(jnp.int32)
    rank = jnp.sum((jnp.cumsum(onehot, axis=0) - onehot) * onehot, axis=1)
    counts = jnp.sum(onehot, axis=0)
    tiles_per = (counts + tm - 1) // tm
    tile_end = jnp.cumsum(tiles_per)
    tile_start = tile_end - tiles_per
    n_active = tile_end[-1]
    dst = tile_start[e_flat] * tm + rank
    tile_ids = jnp.arange(n_tiles, dtype=jnp.int32)
    tile_expert = jnp.minimum(jnp.searchsorted(tile_end, tile_ids, side="right"), n_exp - 1)
    last_expert = tile_expert[jnp.maximum(n_active - 1, 0)]
    tile_rows = jnp.clip(counts[tile_expert] - (tile_ids - tile_start[tile_expert]) * tm, 0, tm)
    tile_rows = jnp.where(tile_ids < n_active, tile_rows, 0).astype(jnp.int32)
    tile_expert = jnp.where(tile_ids < n_active, tile_expert, last_expert).astype(jnp.int32)
    order = jnp.argsort(e_flat, stable=True).astype(jnp.int32)
    group_off = jnp.cumsum(counts) - counts
    slot = jnp.arange(r, dtype=jnp.int32)
    slot_e = tile_expert[slot // tm]
    within = slot - tile_start[slot_e] * tm
    valid = (within < counts[slot_e]) & (slot // tm < n_active)
    src = jnp.where(valid, order[jnp.clip(group_off[slot_e] + within, 0, TOP_K * t - 1)] % t, 0)
    src = src.astype(jnp.int32)

    na = n_active.astype(jnp.int32).reshape(1)
    xg = _gather_rows(h, src, na * (tm // rows_g), rows_g)
    gu = _grouped_gateup(xg, w1, w3, layer, tile_expert, na, tile_rows, tm)
    eo = _grouped_down(gu, w2, tile_expert, na, tile_rows, tm)
    steps = t // rows_c
    dst2 = jnp.concatenate([dst[:t].reshape(steps, 1, rows_c), dst[t:].reshape(steps, 1, rows_c)],
                           axis=2).astype(jnp.int32)
    return _combine(eo, dst2, x, route, rows_c)


def kernel(x_prompt, x_sample, cache_k, cache_v, cache_conv, w_in, w_out, conv_w, attn_norm,
           q_norm, k_norm, lambda_q1, lambda_k1, lambda_q2, lambda_k2, subln, ffn_norm,
           w1_dense, w3_dense, w2_dense, w_router, w1_exp, w3_exp, w2_exp):
    depth = w_in.shape[0]
    nbp, seq, d = x_prompt.shape
    nbs, sseq, _ = x_sample.shape
    past, heads, _, dk = cache_k.shape[2:]
    dv = cache_v.shape[-1]
    cdim = cache_conv.shape[-1]
    hw = 2 * dk
    assert dv == hw == LANES, "one head (both maps / the value row) must span one lane tile"
    qk_dim = heads * hw
    attn_dim = heads * dv
    tp, ts = nbp * seq, nbs * sseq
    assert tp % sseq == 0 and seq % CHUNK == 0
    c_q, c_k, c_v = 0, qk_dim, 2 * qk_dim
    c_b = c_v + attn_dim

    x = jnp.concatenate([x_prompt.reshape(tp, d), x_sample.reshape(ts, d)], axis=0)
    zero_state = jnp.zeros((nbp, 2, cdim), F32)
    ck = cache_k.reshape(depth, nbs, past, qk_dim)
    cv = cache_v.reshape(depth, nbs, past, attn_dim)

    ks, vs, convs_p, convs_s = [], [], [], []
    for l in range(depth):
        lam0 = _lambda_init(l)
        lam_rows = [a[l].reshape(1, dk) for a in (lambda_q1, lambda_k1, lambda_q2, lambda_k2)]
        sub = subln[l].reshape(1, dv)

        h = _rmsnorm(x, attn_norm[l])
        wi = w_in[l].astype(BF16)
        q = _proj_norm(h, wi, c_q, qk_dim, q_norm[l], dk, dk ** -0.5, BF16)
        k = _proj_norm(h, wi, c_k, qk_dim, k_norm[l], dk, 1.0, F32)
        v = _matmul(h, wi, c_v, attn_dim, F32)
        pc = _matmul(h, wi, c_b, 3 * cdim, F32)
        o_p = _attn_prompt(q, k, 0, v, 0, nbp, seq, heads, dk, lam_rows, sub, lam0)
        o_s = _attn_sample(q, k, v, 0, ck, cv, l, tp, nbs, sseq, heads, dk, lam_rows, sub, lam0)
        y_p, nc_p = _conv(pc, 0, cdim, 2 * cdim, cdim, 0, nbp, seq, zero_state, conv_w[l])
        y_s, nc_s = _conv(pc, 0, cdim, 2 * cdim, cdim, tp, nbs, sseq, cache_conv[l], conv_w[l])
        x = _outproj(jnp.concatenate([o_p, o_s], axis=0), jnp.concatenate([y_p, y_s], axis=0),
                     w_out[l].astype(BF16), x)
        ks.append(k)
        vs.append(v)
        convs_p.append(nc_p)
        convs_s.append(nc_s)

        m = l // 2
        if l % 2 == 0:
            hf = _rmsnorm(x, ffn_norm[l])
            gu = _gateup(hf, w1_dense[m].astype(BF16), w3_dense[m].astype(BF16))
            x = _down(gu, w2_dense[m].astype(BF16), x)
        else:
            x = _moe_ffn(x, ffn_norm[l], w_router[m], w1_exp, w3_exp, m, w2_exp[m].astype(BF16))

    k_all = jnp.stack(ks)
    v_all = jnp.stack(vs)
    return (x[:tp].reshape(nbp, seq, d),
            x[tp:].reshape(nbs, sseq, d),
            k_all[:, :tp].reshape(depth, nbp, seq, heads, 2, dk),
            v_all[:, :tp].reshape(depth, nbp, seq, heads, dv),
            jnp.stack(convs_p),
            k_all[:, tp:].reshape(depth, nbs, sseq, heads, 2, dk),
            v_all[:, tp:].reshape(depth, nbs, sseq, heads, dv),
            jnp.stack(convs_s))
```

```python
import functools
import math

import jax
import jax.numpy as jnp
from jax import lax
from jax.experimental import pallas as pl
from jax.experimental.pallas import tpu as pltpu

CHUNK = 64
TOP_K = 2
EPS = 1e-6
NEG_INF = -1e30
F32 = jnp.float32
BF16 = jnp.bfloat16

LANES = 128
MXU_DIM = 256
VMEM_LIMIT = 60000 * 1024

ROW_TILE = 1100
EXPERT_ROW_TILE = 1152
EXPERT_SUB_ROWS = 384
GATHER_ROWS = 256
ATTN_BLOCK = 512
ATTN_KEY_BLOCK = 256
ATTN_HEADS = 2


def _lambda_init(l):
    return 0.8 - 0.6 * math.exp(-0.3 * l)


def _tile(n, pref, mult):
    if n <= pref:
        return n
    best = None
    for t in range(mult, pref + 1, mult):
        if n % t == 0:
            best = t
    assert best is not None, (n, pref, mult)
    return best


def _params(*sem):
    return pltpu.CompilerParams(dimension_semantics=sem, vmem_limit_bytes=VMEM_LIMIT)


def _dot(a, b):
    return jnp.dot(a, b, preferred_element_type=F32)


def _rmsnorm_kernel(x_ref, g_ref, o_ref):
    x = x_ref[...]
    y = x * lax.rsqrt(jnp.mean(x * x, axis=-1, keepdims=True) + EPS)
    o_ref[...] = (y * g_ref[...]).astype(o_ref.dtype)


def _rmsnorm(x, g):
    t, d = x.shape
    tr = _tile(t, 256, 16)
    return pl.pallas_call(
        _rmsnorm_kernel,
        out_shape=jax.ShapeDtypeStruct((t, d), BF16),
        grid=(t // tr,),
        in_specs=[pl.BlockSpec((tr, d), lambda i: (i, 0)),
                  pl.BlockSpec((1, d), lambda i: (0, 0))],
        out_specs=pl.BlockSpec((tr, d), lambda i: (i, 0)),
        compiler_params=_params("parallel"),
        name="rmsnorm",
    )(x, g.reshape(1, d))


def _mm_kernel(a_ref, b_ref, o_ref):
    o_ref[...] = _dot(a_ref[...], b_ref[...]).astype(o_ref.dtype)


def _matmul(a, b, col_off, width, out_dtype, tn_pref=1024):
    m, k = a.shape
    tm = _tile(m, ROW_TILE, 16)
    tn = _tile(math.gcd(width, col_off) if col_off else width, tn_pref, LANES)
    off = col_off // tn
    return pl.pallas_call(
        _mm_kernel,
        out_shape=jax.ShapeDtypeStruct((m, width), out_dtype),
        grid=(m // tm, width // tn),
        in_specs=[pl.BlockSpec((tm, k), lambda i, j: (i, 0)),
                  pl.BlockSpec((k, tn), lambda i, j: (0, j + off))],
        out_specs=pl.BlockSpec((tm, tn), lambda i, j: (i, j)),
        compiler_params=_params("parallel", "parallel"),
        name="in_proj",
    )(a, b)


def _proj_norm_kernel(a_ref, b_ref, g_ref, gm_ref, o_ref, *, group, scale):
    acc = _dot(a_ref[...], b_ref[...])
    gm = gm_ref[...]
    w = gm.shape[0]
    for c in range(0, acc.shape[1], w):
        x = acc[:, c:c + w]
        sq = x * x
        hi = sq.astype(BF16)
        lo = (sq - hi.astype(F32)).astype(BF16)
        ssq = _dot(hi, gm) + _dot(lo, gm)
        y = x * lax.rsqrt(ssq * (1.0 / group) + EPS)
        o_ref[:, c:c + w] = ((y * g_ref[...]) * scale).astype(o_ref.dtype)


def _proj_norm(a, b, col_off, width, gain, group, scale, out_dtype):
    m, k = a.shape
    gw = MXU_DIM
    tm = _tile(m, ROW_TILE, 16)
    tn = _tile(math.gcd(width, col_off) if col_off else width, 512, gw)
    assert gw % group == 0
    ids = jnp.arange(gw) // group
    gm = (ids[:, None] == ids[None, :]).astype(BF16)
    g = jnp.tile(gain.astype(F32), gw // group).reshape(1, gw)
    off = col_off // tn
    return pl.pallas_call(
        functools.partial(_proj_norm_kernel, group=group, scale=scale),
        out_shape=jax.ShapeDtypeStruct((m, width), out_dtype),
        grid=(m // tm, width // tn),
        in_specs=[pl.BlockSpec((tm, k), lambda i, j: (i, 0)),
                  pl.BlockSpec((k, tn), lambda i, j: (0, j + off)),
                  pl.BlockSpec((1, gw), lambda i, j: (0, 0)),
                  pl.BlockSpec((gw, gw), lambda i, j: (0, 0))],
        out_specs=pl.BlockSpec((tm, tn), lambda i, j: (i, j)),
        compiler_params=_params("parallel", "parallel"),
        name="qk_proj_norm",
    )(a, b, g, gm)


def _diff_lambda(lq1, lk1, lq2, lk2, lam_init):
    a = jnp.exp(jnp.sum(lq1[...] * lk1[...], axis=-1, keepdims=True))
    b = jnp.exp(jnp.sum(lq2[...] * lk2[...], axis=-1, keepdims=True))
    return a - b + lam_init


def _finish_heads(a1, l1, a2, l2, lam, sub, lam_init):
    o = a1 / l1 - lam * (a2 / l2)
    y = o * lax.rsqrt(jnp.mean(o * o, axis=-1, keepdims=True) + EPS)
    return (y * sub) * (1.0 - lam_init)


def _split_maps(q, dk):
    lane = lax.broadcasted_iota(jnp.int32, q.shape, 1)
    zero = jnp.zeros_like(q)
    return jnp.where(lane < dk, q, zero), jnp.where(lane >= dk, q, zero)


def _qk(q, k):
    return lax.dot_general(q, k, (((1,), (1,)), ((), ())), preferred_element_type=F32)


def _attn_prompt_kernel(q_ref, k_ref, v_ref, lq1, lk1, lq2, lk2, sub_ref, o_ref,
                        kb, vt, qt, m_s, l_s, acc, s_ref, p_ref, *, tq, dk, lam_init):
    qi = pl.program_id(2)
    hb, nk, tk = kb.shape[0], kb.shape[1], kb.shape[2]
    hw = 2 * dk
    kpq = tq // tk

    @pl.when(qi == 0)
    def _():
        for hh in range(hb):
            for jj in range(nk):
                kv = (slice(jj * tk, (jj + 1) * tk), slice(hh * hw, (hh + 1) * hw))
                kb[hh, jj] = k_ref[kv].astype(BF16)
                vt[hh, jj] = v_ref[kv].T.astype(BF16)

    for hh in range(hb):
        q = q_ref[:, hh * hw:(hh + 1) * hw].astype(F32)
        lane = lax.broadcasted_iota(jnp.int32, q.shape, 1)
        qt[hh, :, 0:tq] = jnp.where(lane < dk, q, 0.0).T.astype(BF16)
        qt[hh, :, tq:2 * tq] = jnp.where(lane >= dk, q, 0.0).T.astype(BF16)
    m_s[...] = jnp.full(m_s.shape, NEG_INF, F32)
    l_s[...] = jnp.zeros(l_s.shape, F32)
    acc[...] = jnp.zeros(acc.shape, F32)

    def block(j, key0):
        for hh in range(hb):
            s_ref[hh] = _dot(kb[hh, j], qt[hh])
        for hh in range(hb):
            for c in range(0, 2 * tq, LANES):
                if key0 is not None and (c % tq) + LANES <= key0:
                    p_ref[hh, :, c:c + LANES] = jnp.zeros((tk, LANES), BF16)
                    continue
                s = s_ref[hh, :, c:c + LANES]
                if key0 is not None:
                    key = lax.broadcasted_iota(jnp.int32, s.shape, 0) + key0
                    qry = lax.broadcasted_iota(jnp.int32, s.shape, 1) + (c % tq)
                    s = jnp.where(key // CHUNK <= qry // CHUNK, s, NEG_INF)
                m_old = m_s[hh, :, c:c + LANES]
                m_new = jnp.maximum(m_old, jnp.max(s, axis=0, keepdims=True))
                alpha = jnp.exp(m_old - m_new)
                p = jnp.exp(s - m_new)
                l_s[hh, :, c:c + LANES] = (alpha * l_s[hh, :, c:c + LANES]
                                           + jnp.sum(p, axis=0, keepdims=True))
                m_s[hh, :, c:c + LANES] = m_new
                p_ref[hh, :, c:c + LANES] = p.astype(BF16)
                acc[hh, :, c:c + LANES] = alpha * acc[hh, :, c:c + LANES]
            acc[hh] += _dot(vt[hh, j], p_ref[hh])

    def body(j, c):
        block(j, None)
        return c

    lax.fori_loop(0, qi * kpq, body, 0)
    for d in range(kpq):
        block(qi * kpq + d, d * tk)
    lam = _diff_lambda(lq1, lk1, lq2, lk2, lam_init)
    for hh in range(hb):
        r = acc[hh] / l_s[hh]
        o = (r[:, 0:tq] - lam * r[:, tq:2 * tq]).T
        y = o * lax.rsqrt(jnp.mean(o * o, axis=-1, keepdims=True) + EPS)
        o_ref[:, hh * hw:(hh + 1) * hw] = ((y * sub_ref[...])
                                           * (1.0 - lam_init)).astype(o_ref.dtype)


def _attn_prompt(q, kv_src, k_col, v_src, v_col, nb, seq, heads, dk, lam_rows, sub, lam_init):
    hw = 2 * dk
    tq = _tile(seq, ATTN_BLOCK, LANES)
    tk = _tile(tq, ATTN_KEY_BLOCK, CHUNK)
    nq, nk = seq // tq, seq // tk
    hb = next(n for n in (ATTN_HEADS, 1) if heads % n == 0)
    bw = hb * hw
    assert (k_col * hw) % bw == 0 and (v_col * hw) % bw == 0
    kc, vc = k_col * hw // bw, v_col * hw // bw
    vec = lambda i, h, j: (0, 0)
    return pl.pallas_call(
        functools.partial(_attn_prompt_kernel, tq=tq, dk=dk, lam_init=lam_init),
        out_shape=jax.ShapeDtypeStruct((nb * seq, heads * hw), BF16),
        grid=(nb, heads // hb, nq),
        in_specs=[pl.BlockSpec((tq, bw), lambda b, h, j: (b * nq + j, h)),
                  pl.BlockSpec((seq, bw), lambda b, h, j: (b, kc + h)),
                  pl.BlockSpec((seq, bw), lambda b, h, j: (b, vc + h)),
                  pl.BlockSpec((1, dk), vec), pl.BlockSpec((1, dk), vec),
                  pl.BlockSpec((1, dk), vec), pl.BlockSpec((1, dk), vec),
                  pl.BlockSpec((1, hw), vec)],
        out_specs=pl.BlockSpec((tq, bw), lambda b, h, j: (b * nq + j, h)),
        scratch_shapes=[pltpu.VMEM((hb, nk, tk, hw), BF16), pltpu.VMEM((hb, nk, hw, tk), BF16),
                        pltpu.VMEM((hb, hw, 2 * tq), BF16), pltpu.VMEM((hb, 1, 2 * tq), F32),
                        pltpu.VMEM((hb, 1, 2 * tq), F32), pltpu.VMEM((hb, hw, 2 * tq), F32),
                        pltpu.VMEM((hb, tk, 2 * tq), F32), pltpu.VMEM((hb, tk, 2 * tq), BF16)],
        compiler_params=_params("parallel", "parallel", "arbitrary"),
        name="attn_prompt",
    )(q, kv_src, v_src, *lam_rows, sub)


def _attn_sample_kernel(q_ref, kn_ref, vn_ref, kp_ref, vp_ref, lq1, lk1, lq2, lk2, sub_ref, o_ref,
                        *, dk, lam_init):
    hw = 2 * dk
    lam = _diff_lambda(lq1, lk1, lq2, lk2, lam_init)
    for c in range(0, q_ref.shape[1], hw):
        q1, q2 = _split_maps(q_ref[:, c:c + hw], dk)
        kp = kp_ref[:, c:c + hw].astype(BF16)
        vp = vp_ref[:, c:c + hw].astype(BF16)
        kn = kn_ref[:, c:c + hw].astype(BF16)
        vn = vn_ref[:, c:c + hw].astype(BF16)

        def one_map(q):
            sp = _qk(q, kp)
            sn = _qk(q, kn)
            m = jnp.maximum(jnp.max(sp, axis=-1, keepdims=True),
                            jnp.max(sn, axis=-1, keepdims=True))
            ep = jnp.exp(sp - m)
            en = jnp.exp(sn - m)
            l = jnp.sum(ep, axis=-1, keepdims=True) + jnp.sum(en, axis=-1, keepdims=True)
            a = _dot(ep.astype(BF16), vp) + _dot(en.astype(BF16), vn)
            return l, a

        l1, a1 = one_map(q1)
        l2, a2 = one_map(q2)
        o_ref[:, c:c + hw] = _finish_heads(a1, l1, a2, l2, lam, sub_ref[...],
                                           lam_init).astype(o_ref.dtype)


def _attn_sample(q, k_new, v_src, v_col, past_k, past_v, layer, row_off, nb, seq, heads, dk,
                 lam_rows, sub, lam_init):
    hw = 2 * dk
    past = past_k.shape[2]
    rb = row_off // seq
    hb = next(n for n in (4, 2, 1) if heads % n == 0)
    bw = hb * hw
    assert (v_col * hw) % bw == 0
    vc = v_col * hw // bw
    vec = lambda b, h: (0, 0)
    return pl.pallas_call(
        functools.partial(_attn_sample_kernel, dk=dk, lam_init=lam_init),
        out_shape=jax.ShapeDtypeStruct((nb * seq, heads * hw), BF16),
        grid=(nb, heads // hb),
        in_specs=[pl.BlockSpec((seq, bw), lambda b, h: (rb + b, h)),
                  pl.BlockSpec((seq, bw), lambda b, h: (rb + b, h)),
                  pl.BlockSpec((seq, bw), lambda b, h: (rb + b, vc + h)),
                  pl.BlockSpec((None, None, past, bw), lambda b, h: (layer, b, 0, h)),
                  pl.BlockSpec((None, None, past, bw), lambda b, h: (layer, b, 0, h)),
                  pl.BlockSpec((1, dk), vec), pl.BlockSpec((1, dk), vec),
                  pl.BlockSpec((1, dk), vec), pl.BlockSpec((1, dk), vec),
                  pl.BlockSpec((1, hw), vec)],
        out_specs=pl.BlockSpec((seq, bw), lambda b, h: (b, h)),
        compiler_params=_params("parallel", "parallel"),
        name="attn_sample",
    )(q, k_new, v_src, past_k, past_v, *lam_rows, sub)


def _conv_kernel(bg_ref, cg_ref, hc_ref, st_ref, cw_ref, y_ref, nc_ref, carry, *, ts):
    s = pl.program_id(2)
    u = cg_ref[...] * hc_ref[...]

    @pl.when(s == 0)
    def _():
        carry[6:8, :] = st_ref[...]

    prev = carry[...]
    p2 = prev[6:7, :]
    p1 = prev[7:8, :]
    row = lax.broadcasted_iota(jnp.int32, u.shape, 0)
    u1 = jnp.where(row == 0, p1, pltpu.roll(u, 1, 0))
    u2 = jnp.where(row == 0, p2, jnp.where(row == 1, p1, pltpu.roll(u, 2, 0)))
    cw = cw_ref[...]
    y = bg_ref[...] * (cw[0:1, :] * u2 + cw[1:2, :] * u1 + cw[2:3, :] * u)
    y_ref[...] = y.astype(y_ref.dtype)
    carry[...] = u[ts - 8:ts, :]

    @pl.when(s == pl.num_programs(2) - 1)
    def _():
        nc_ref[...] = u[ts - 2:ts, :]


def _conv(p, col_b, col_c, col_h, width, row_off, nb, seq, state, cw):
    assert seq >= 8 and state.shape[1] == 2 and cw.shape[0] == 3
    ts = _tile(seq, 512, 8)
    tc = _tile(width, 512, LANES)
    ns = seq // ts
    rb = row_off // ts
    ob, oc, oh = col_b // tc, col_c // tc, col_h // tc
    blk = lambda o: pl.BlockSpec((ts, tc), lambda b, c, s: (rb + b * ns + s, o + c))
    return pl.pallas_call(
        functools.partial(_conv_kernel, ts=ts),
        out_shape=(jax.ShapeDtypeStruct((nb * seq, width), BF16),
                   jax.ShapeDtypeStruct((nb, 2, width), F32)),
        grid=(nb, width // tc, ns),
        in_specs=[blk(ob), blk(oc), blk(oh),
                  pl.BlockSpec((None, 2, tc), lambda b, c, s: (b, 0, c)),
                  pl.BlockSpec((3, tc), lambda b, c, s: (0, c))],
        out_specs=(pl.BlockSpec((ts, tc), lambda b, c, s: (b * ns + s, c)),
                   pl.BlockSpec((None, 2, tc), lambda b, c, s: (b, 0, c))),
        scratch_shapes=[pltpu.VMEM((8, tc), F32)],
        compiler_params=_params("parallel", "parallel", "arbitrary"),
        name="short_conv",
    )(p, p, p, state, cw)


def _outproj_kernel(a1_ref, a2_ref, b1_ref, b2_ref, x_ref, o_ref):
    o_ref[...] = x_ref[...] + (_dot(a1_ref[...], b1_ref[...]) + _dot(a2_ref[...], b2_ref[...]))


def _outproj(a1, a2, w, x):
    m, k1 = a1.shape
    k2 = a2.shape[1]
    assert k1 == k2 and w.shape[0] == k1 + k2
    n = w.shape[1]
    tm = _tile(m, ROW_TILE, 16)
    tn = _tile(n, 512, LANES)
    return pl.pallas_call(
        _outproj_kernel,
        out_shape=jax.ShapeDtypeStruct((m, n), F32),
        grid=(m // tm, n // tn),
        in_specs=[pl.BlockSpec((tm, k1), lambda i, j: (i, 0)),
                  pl.BlockSpec((tm, k2), lambda i, j: (i, 0)),
                  pl.BlockSpec((k1, tn), lambda i, j: (0, j)),
                  pl.BlockSpec((k2, tn), lambda i, j: (1, j)),
                  pl.BlockSpec((tm, tn), lambda i, j: (i, j))],
        out_specs=pl.BlockSpec((tm, tn), lambda i, j: (i, j)),
        compiler_params=_params("parallel", "parallel"),
        name="out_proj",
    )(a1, a2, w, w, x)


def _silu_mul(g, u):
    return (g / (1.0 + jnp.exp(-g))) * u


def _gateup_kernel(a_ref, w1_ref, w3_ref, o_ref):
    a = a_ref[...]
    o_ref[...] = _silu_mul(_dot(a, w1_ref[...]), _dot(a, w3_ref[...])).astype(o_ref.dtype)


def _gateup_cast_kernel(a_ref, w1_ref, w3_ref, c_ref, o_ref, co_ref):
    _gateup_kernel(a_ref, w1_ref, w3_ref, o_ref)
    co_ref[...] = c_ref[...].astype(co_ref.dtype)


def _gateup(a, w1, w3, cast_src=None, cast_layer=0):
    m, k = a.shape
    n = w1.shape[1]
    tm = _tile(m, ROW_TILE, 16)
    tn = _tile(n, 512, LANES)
    nj = n // tn
    steps = (m // tm) * nj
    in_specs = [pl.BlockSpec((tm, k), lambda i, j: (i, 0)),
                pl.BlockSpec((k, tn), lambda i, j: (0, j)),
                pl.BlockSpec((k, tn), lambda i, j: (0, j))]
    out_spec = pl.BlockSpec((tm, tn), lambda i, j: (i, j))
    out_shape = jax.ShapeDtypeStruct((m, n), BF16)
    if cast_src is None:
        return pl.pallas_call(
            _gateup_kernel, out_shape=out_shape, grid=(m // tm, nj), in_specs=in_specs,
            out_specs=out_spec, compiler_params=_params("parallel", "parallel"),
            name="ffn_gate_up",
        )(a, w1, w3), None
    _, cr, cc = cast_src.shape
    assert cr % steps == 0 and (cr // steps) % 16 == 0, "cast rows must split evenly over steps"
    cb = cr // steps
    return pl.pallas_call(
        _gateup_cast_kernel,
        out_shape=(out_shape, jax.ShapeDtypeStruct((cr, cc), BF16)),
        grid=(m // tm, nj),
        in_specs=in_specs + [pl.BlockSpec((None, cb, cc),
                                          lambda i, j: (cast_layer, i * nj + j, 0))],
        out_specs=(out_spec, pl.BlockSpec((cb, cc), lambda i, j: (i * nj + j, 0))),
        compiler_params=_params("parallel", "parallel"),
        name="ffn_gate_up",
    )(a, w1, w3, cast_src)


def _down_kernel(a_ref, b_ref, x_ref, o_ref):
    d = _dot(a_ref[...], b_ref[...])

    @pl.when(pl.program_id(2) == 0)
    def _():
        o_ref[...] = x_ref[...] + d

    @pl.when(pl.program_id(2) > 0)
    def _():
        o_ref[...] += d


def _down(a, w, x):
    m, k = a.shape
    n = w.shape[1]
    tm = _tile(m, ROW_TILE, 16)
    tn = _tile(n, 1024, LANES)
    tk = _tile(k, 2048, MXU_DIM)
    return pl.pallas_call(
        _down_kernel,
        out_shape=jax.ShapeDtypeStruct((m, n), F32),
        grid=(m // tm, n // tn, k // tk),
        in_specs=[pl.BlockSpec((tm, tk), lambda i, j, kk: (i, kk)),
                  pl.BlockSpec((tk, tn), lambda i, j, kk: (kk, j)),
                  pl.BlockSpec((tm, tn), lambda i, j, kk: (i, j))],
        out_specs=pl.BlockSpec((tm, tn), lambda i, j, kk: (i, j)),
        compiler_params=_params("parallel", "parallel", "arbitrary"),
        name="ffn_down",
    )(a, w, x)


def _grouped_gateup_kernel(te_ref, na_ref, rows_ref, first_ref, a_ref, w1_ref, w3_ref, o_ref,
                           wb1, wb3, *, sub):
    i = pl.program_id(1)
    rows = rows_ref[i]
    tm = a_ref.shape[0]

    @pl.when(first_ref[i] == 1)
    def _():
        wb1[...] = w1_ref[...].astype(BF16)
        wb3[...] = w3_ref[...].astype(BF16)

    def compute(lo, size):
        a = a_ref[lo:lo + size, :]
        g = _dot(a, wb1[...])
        u = _dot(a, wb3[...])
        o_ref[lo:lo + size, :] = _silu_mul(g, u).astype(o_ref.dtype)

    @pl.when(rows == tm)
    def _():
        compute(0, tm)

    for s in range(tm // sub):
        @pl.when((rows < tm) & (s * sub < rows))
        def _():
            compute(s * sub, sub)

        @pl.when((rows < tm) & (s * sub >= rows))
        def _():
            o_ref[s * sub:(s + 1) * sub, :] = jnp.zeros((sub, o_ref.shape[1]), o_ref.dtype)


def _grouped_gateup(xg, w1, w3, layer, tile_expert, n_active, tile_rows, tile_first, tm):
    r, k = xg.shape
    n = w1.shape[3]
    tn = _tile(n, 256, LANES)
    sub = _tile(tm, EXPERT_SUB_ROWS, 16)
    wspec = pl.BlockSpec((None, None, k, tn), lambda j, i, te, na, tr, tf: (layer, te[i], 0, j))
    return pl.pallas_call(
        functools.partial(_grouped_gateup_kernel, sub=sub),
        out_shape=jax.ShapeDtypeStruct((r, n), BF16),
        grid_spec=pltpu.PrefetchScalarGridSpec(
            num_scalar_prefetch=4,
            grid=(n // tn, r // tm),
            in_specs=[pl.BlockSpec((tm, k),
                                   lambda j, i, te, na, tr, tf: (jnp.minimum(i, na[0] - 1), 0)),
                      wspec, wspec],
            out_specs=pl.BlockSpec((tm, tn), lambda j, i, te, na, tr, tf: (i, j)),
            scratch_shapes=[pltpu.VMEM((k, tn), BF16), pltpu.VMEM((k, tn), BF16)]),
        compiler_params=_params("arbitrary", "arbitrary"),
        name="moe_gate_up",
    )(tile_expert, n_active, tile_rows, tile_first, xg, w1, w3)


def _grouped_down_kernel(te_ref, na_ref, rows_ref, a_ref, b_ref, o_ref, *, sub):
    rows = rows_ref[pl.program_id(0)]
    first = pl.program_id(2) == 0
    tm = a_ref.shape[0]

    def compute(lo, size):
        d = _dot(a_ref[lo:lo + size, :], b_ref[...])

        @pl.when(first)
        def _():
            o_ref[lo:lo + size, :] = d

        @pl.when(jnp.logical_not(first))
        def _():
            o_ref[lo:lo + size, :] += d

    @pl.when(rows == tm)
    def _():
        compute(0, tm)

    for s in range(tm // sub):
        @pl.when((rows < tm) & (s * sub < rows))
        def _():
            compute(s * sub, sub)

        @pl.when((rows < tm) & (s * sub >= rows) & first)
        def _():
            o_ref[s * sub:(s + 1) * sub, :] = jnp.zeros((sub, o_ref.shape[1]), o_ref.dtype)


def _grouped_down(g, w, tile_expert, n_active, tile_rows, tm):
    r, k = g.shape
    n = w.shape[2]
    tn = _tile(n, 2048, LANES)
    tk = _tile(k, 1024, MXU_DIM)
    nj, nk = n // tn, k // tk
    sub = _tile(tm, EXPERT_SUB_ROWS, 16)

    def a_map(i, j, kk, te, na, tr):
        act = i < na[0]
        return jnp.minimum(i, na[0] - 1), jnp.where(act, kk, nk - 1)

    def b_map(i, j, kk, te, na, tr):
        act = i < na[0]
        return te[i], jnp.where(act, kk, nk - 1), jnp.where(act, j, nj - 1)

    return pl.pallas_call(
        functools.partial(_grouped_down_kernel, sub=sub),
        out_shape=jax.ShapeDtypeStruct((r, n), F32),
        grid_spec=pltpu.PrefetchScalarGridSpec(
            num_scalar_prefetch=3,
            grid=(r // tm, nj, nk),
            in_specs=[pl.BlockSpec((tm, tk), a_map),
                      pl.BlockSpec((None, tk, tn), b_map)],
            out_specs=pl.BlockSpec((tm, tn), lambda i, j, kk, te, na, tr: (i, j))),
        compiler_params=_params("parallel", "parallel", "arbitrary"),
        name="moe_down",
    )(tile_expert, n_active, tile_rows, g, w)


def _router_kernel(x_ref, g_ref, wr_ref, h_ref, r_ref, *, n_exp):
    x = x_ref[...]
    h = (x * lax.rsqrt(jnp.mean(x * x, axis=-1, keepdims=True) + EPS)) * g_ref[...]
    h_ref[...] = h
    logits = jnp.dot(h, wr_ref[...], precision=lax.Precision.HIGHEST, preferred_element_type=F32)
    lane = lax.broadcasted_iota(jnp.int32, logits.shape, 1)
    lanef = lane.astype(F32)
    lg = jnp.where(lane < n_exp, logits, -jnp.inf)
    m1 = jnp.max(lg, axis=-1, keepdims=True)
    i1 = jnp.min(jnp.where(lg == m1, lanef, float(LANES)), axis=-1, keepdims=True)
    lg2 = jnp.where(lanef == i1, -jnp.inf, lg)
    m2 = jnp.max(lg2, axis=-1, keepdims=True)
    i2 = jnp.min(jnp.where(lg2 == m2, lanef, float(LANES)), axis=-1, keepdims=True)
    e = jnp.exp(m2 - m1)
    den = 1.0 + e
    r_ref[...] = jnp.where(lane == 0, i1,
                 jnp.where(lane == 1, i2,
                 jnp.where(lane == 2, 1.0 / den,
                 jnp.where(lane == 3, e / den, 0.0))))


def _router(x, g, w_router):
    t, d = x.shape
    n_exp = w_router.shape[1]
    assert n_exp <= LANES and TOP_K == 2
    wr = jnp.zeros((d, LANES), F32).at[:, :n_exp].set(w_router.astype(F32))
    tr = _tile(t, 256, 8)
    return pl.pallas_call(
        functools.partial(_router_kernel, n_exp=n_exp),
        out_shape=(jax.ShapeDtypeStruct((t, d), F32), jax.ShapeDtypeStruct((t, LANES), F32)),
        grid=(t // tr,),
        in_specs=[pl.BlockSpec((tr, d), lambda i: (i, 0)),
                  pl.BlockSpec((1, d), lambda i: (0, 0)),
                  pl.BlockSpec((d, LANES), lambda i: (0, 0))],
        out_specs=(pl.BlockSpec((tr, d), lambda i: (i, 0)),
                   pl.BlockSpec((tr, LANES), lambda i: (i, 0))),
        compiler_params=_params("parallel"),
        name="ffn_norm_router",
    )(x, g.reshape(1, d), wr)


def _row_copy(src_hbm, idx, dst_vmem, row, sem):
    return pltpu.make_async_copy(src_hbm.at[pl.ds(idx, 1), :], dst_vmem.at[pl.ds(row, 1), :], sem)


def _gather_kernel(na_ref, src_ref, nxt_ref, h_hbm, o_ref, buf, sem, *, rows):
    i = pl.program_id(0)
    slot = i % 2

    def issue(idx_ref, sl):
        def body(r, c):
            _row_copy(h_hbm, idx_ref[0, 0, r], buf.at[sl], r, sem.at[sl]).start()
            return c
        lax.fori_loop(0, rows, body, 0, unroll=8)

    @pl.when(i == 0)
    def _():
        issue(src_ref, 0)

    @pl.when(i + 1 < na_ref[0])
    def _():
        issue(nxt_ref, 1 - slot)

    @pl.when(i < na_ref[0])
    def _():
        def wait(r, c):
            _row_copy(h_hbm, src_ref[0, 0, r], buf.at[slot], r, sem.at[slot]).wait()
            return c
        lax.fori_loop(0, rows, wait, 0, unroll=8)
        o_ref[...] = buf[slot].astype(o_ref.dtype)

    @pl.when(i >= na_ref[0])
    def _():
        o_ref[...] = jnp.zeros_like(o_ref)


def _gather_rows(h, src, n_active_steps, rows):
    r = src.shape[0]
    d = h.shape[1]
    steps = r // rows
    src3 = src.reshape(steps, 1, rows)
    idx_spec = lambda f: pl.BlockSpec((1, 1, rows), f, memory_space=pltpu.SMEM)
    return pl.pallas_call(
        functools.partial(_gather_kernel, rows=rows),
        out_shape=jax.ShapeDtypeStruct((r, d), BF16),
        grid_spec=pltpu.PrefetchScalarGridSpec(
            num_scalar_prefetch=1,
            grid=(steps,),
            in_specs=[idx_spec(lambda i, na: (i, 0, 0)),
                      idx_spec(lambda i, na: (jnp.minimum(i + 1, steps - 1), 0, 0)),
                      pl.BlockSpec(memory_space=pl.ANY)],
            out_specs=pl.BlockSpec((rows, d), lambda i, na: (i, 0)),
            scratch_shapes=[pltpu.VMEM((2, rows, d), F32), pltpu.SemaphoreType.DMA((2,))]),
        compiler_params=_params("arbitrary"),
        name="moe_gather",
    )(n_active_steps, src3, src3, h)


def _combine_kernel(dst_ref, eo_hbm, x_ref, r_ref, o_ref, buf, sem, *, rows):
    def start(r, c):
        _row_copy(eo_hbm, dst_ref[0, 0, r], buf.at[0], r, sem.at[0]).start()
        _row_copy(eo_hbm, dst_ref[0, 0, rows + r], buf.at[1], r, sem.at[0]).start()
        return c

    def wait(r, c):
        _row_copy(eo_hbm, dst_ref[0, 0, r], buf.at[0], r, sem.at[0]).wait()
        _row_copy(eo_hbm, dst_ref[0, 0, rows + r], buf.at[1], r, sem.at[0]).wait()
        return c

    lax.fori_loop(0, rows, start, 0, unroll=8)
    lax.fori_loop(0, rows, wait, 0, unroll=8)
    route = r_ref[...]
    o_ref[...] = x_ref[...] + (route[:, 2:3] * buf[0] + route[:, 3:4] * buf[1])


def _combine(eo, dst, x, route, rows):
    t, d = x.shape
    steps = t // rows
    return pl.pallas_call(
        functools.partial(_combine_kernel, rows=rows),
        out_shape=jax.ShapeDtypeStruct((t, d), F32),
        grid=(steps,),
        in_specs=[pl.BlockSpec((1, 1, 2 * rows), lambda i: (i, 0, 0), memory_space=pltpu.SMEM),
                  pl.BlockSpec(memory_space=pl.ANY),
                  pl.BlockSpec((rows, d), lambda i: (i, 0)),
                  pl.BlockSpec((rows, LANES), lambda i: (i, 0))],
        out_specs=pl.BlockSpec((rows, d), lambda i: (i, 0)),
        scratch_shapes=[pltpu.VMEM((2, rows, d), F32), pltpu.SemaphoreType.DMA((1,))],
        compiler_params=_params("arbitrary"),
        name="moe_combine",
    )(dst, eo, x, route)


def _moe_ffn(x, g, w_router, w1, w3, layer, w2):
    t, d = x.shape
    n_exp = w_router.shape[1]
    tm = EXPERT_ROW_TILE
    rows_g = _tile(tm, GATHER_ROWS, 8)
    rows_c = _tile(t, GATHER_ROWS, 8)
    n_tiles = -(-(TOP_K * t) // tm) + n_exp
    r = n_tiles * tm

    h, route = _router(x, g, w_router)

    e_flat = jnp.concatenate([route[:, 0], route[:, 1]]).astype(jnp.int32)
    onehot = (e_flat[:, None] == jnp.arange(n_exp, dtype=jnp.int32)[None, :]).astype(jnp.int32)
    rank = jnp.sum((jnp.cumsum(onehot, axis=0) - onehot) * onehot, axis=1)
    counts = jnp.sum(onehot, axis=0)
    tiles_per = (counts + tm - 1) // tm
    tile_end = jnp.cumsum(tiles_per)
    tile_start = tile_end - tiles_per
    n_active = tile_end[-1]
    dst = tile_start[e_flat] * tm + rank
    tile_ids = jnp.arange(n_tiles, dtype=jnp.int32)
    tile_expert = jnp.minimum(jnp.searchsorted(tile_end, tile_ids, side="right"), n_exp - 1)
    last_expert = tile_expert[jnp.maximum(n_active - 1, 0)]
    tile_rows = jnp.clip(counts[tile_expert] - (tile_ids - tile_start[tile_expert]) * tm, 0, tm)
    tile_rows = jnp.where(tile_ids < n_active, tile_rows, 0).astype(jnp.int32)
    tile_first = ((tile_ids == tile_start[tile_expert]) & (tile_ids < n_active)).astype(jnp.int32)
    tile_expert = jnp.where(tile_ids < n_active, tile_expert, last_expert).astype(jnp.int32)
    order = jnp.argsort(e_flat, stable=True).astype(jnp.int32)
    group_off = jnp.cumsum(counts) - counts
    slot = jnp.arange(r, dtype=jnp.int32)
    slot_e = tile_expert[slot // tm]
    within = slot - tile_start[slot_e] * tm
    valid = (within < counts[slot_e]) & (slot // tm < n_active)
    src = jnp.where(valid, order[jnp.clip(group_off[slot_e] + within, 0, TOP_K * t - 1)] % t, 0)
    src = src.astype(jnp.int32)

    na = n_active.astype(jnp.int32).reshape(1)
    xg = _gather_rows(h, src, na * (tm // rows_g), rows_g)
    gu = _grouped_gateup(xg, w1, w3, layer, tile_expert, na, tile_rows, tile_first, tm)
    eo = _grouped_down(gu, w2, tile_expert, na, tile_rows, tm)
    steps = t // rows_c
    dst2 = jnp.concatenate([dst[:t].reshape(steps, 1, rows_c), dst[t:].reshape(steps, 1, rows_c)],
                           axis=2).astype(jnp.int32)
    return _combine(eo, dst2, x, route, rows_c)


def kernel(x_prompt, x_sample, cache_k, cache_v, cache_conv, w_in, w_out, conv_w, attn_norm,
           q_norm, k_norm, lambda_q1, lambda_k1, lambda_q2, lambda_k2, subln, ffn_norm,
           w1_dense, w3_dense, w2_dense, w_router, w1_exp, w3_exp, w2_exp):
    depth = w_in.shape[0]
    nbp, seq, d = x_prompt.shape
    nbs, sseq, _ = x_sample.shape
    past, heads, _, dk = cache_k.shape[2:]
    dv = cache_v.shape[-1]
    cdim = cache_conv.shape[-1]
    hw = 2 * dk
    assert dv == hw == LANES, "one head (both maps / the value row) must span one lane tile"
    qk_dim = heads * hw
    attn_dim = heads * dv
    tp, ts = nbp * seq, nbs * sseq
    assert tp % sseq == 0 and seq % CHUNK == 0
    c_q, c_k, c_v = 0, qk_dim, 2 * qk_dim
    c_b = c_v + attn_dim

    x = jnp.concatenate([x_prompt.reshape(tp, d), x_sample.reshape(ts, d)], axis=0)
    zero_state = jnp.zeros((nbp, 2, cdim), F32)
    ck = cache_k.reshape(depth, nbs, past, qk_dim)
    cv = cache_v.reshape(depth, nbs, past, attn_dim)

    ks, vs, convs_p, convs_s = [], [], [], []
    for l in range(depth):
        lam0 = _lambda_init(l)
        lam_rows = [a[l].reshape(1, dk) for a in (lambda_q1, lambda_k1, lambda_q2, lambda_k2)]
        sub = subln[l].reshape(1, dv)

        h = _rmsnorm(x, attn_norm[l])
        wi = w_in[l].astype(BF16)
        q = _proj_norm(h, wi, c_q, qk_dim, q_norm[l], dk, dk ** -0.5, BF16)
        k = _proj_norm(h, wi, c_k, qk_dim, k_norm[l], dk, 1.0, F32)
        v = _matmul(h, wi, c_v, attn_dim, F32)
        pc = _matmul(h, wi, c_b, 3 * cdim, F32)
        o_p = _attn_prompt(q, k, 0, v, 0, nbp, seq, heads, dk, lam_rows, sub, lam0)
        o_s = _attn_sample(q, k, v, 0, ck, cv, l, tp, nbs, sseq, heads, dk, lam_rows, sub, lam0)
        y_p, nc_p = _conv(pc, 0, cdim, 2 * cdim, cdim, 0, nbp, seq, zero_state, conv_w[l])
        y_s, nc_s = _conv(pc, 0, cdim, 2 * cdim, cdim, tp, nbs, sseq, cache_conv[l], conv_w[l])
        x = _outproj(jnp.concatenate([o_p, o_s], axis=0), jnp.concatenate([y_p, y_s], axis=0),
                     w_out[l].astype(BF16), x)
        ks.append(k)
        vs.append(v)
        convs_p.append(nc_p)
        convs_s.append(nc_s)

        m = l // 2
        if l % 2 == 0:
            hf = _rmsnorm(x, ffn_norm[l])
            nxt = w2_exp.reshape(w2_exp.shape[0], -1, d) if l + 1 < depth else None
            gu, w2_next = _gateup(hf, w1_dense[m].astype(BF16), w3_dense[m].astype(BF16), nxt, m)
            x = _down(gu, w2_dense[m].astype(BF16), x)
        else:
            w2 = w2_next.reshape(w2_exp.shape[1:])
            x = _moe_ffn(x, ffn_norm[l], w_router[m], w1_exp, w3_exp, m, w2)

    k_all = jnp.stack(ks)
    v_all = jnp.stack(vs)
    return (x[:tp].reshape(nbp, seq, d),
            x[tp:].reshape(nbs, sseq, d),
            k_all[:, :tp].reshape(depth, nbp, seq, heads, 2, dk),
            v_all[:, :tp].reshape(depth, nbp, seq, heads, dv),
            jnp.stack(convs_p),
            k_all[:, tp:].reshape(depth, nbs, sseq, heads, 2, dk),
            v_all[:, tp:].reshape(depth, nbs, sseq, heads, dv),
            jnp.stack(convs_s))
```

```python
import functools
import math

import jax
import jax.numpy as jnp
from jax import lax
from jax.experimental import pallas as pl
from jax.experimental.pallas import tpu as pltpu

CHUNK = 64
TOP_K = 2
EPS = 1e-6
NEG_INF = -1e30
F32 = jnp.float32
BF16 = jnp.bfloat16

LANES = 128
MXU_DIM = 256
VMEM_LIMIT = 60000 * 1024

ROW_TILE = 1100
EXPERT_ROW_TILE = 1536
EXPERT_SUB_ROWS = 384
GATHER_ROWS = 256
ATTN_BLOCK = 512
ATTN_KEY_BLOCK = 256
ATTN_HEADS = 2


def _lambda_init(l):
    return 0.8 - 0.6 * math.exp(-0.3 * l)


def _tile(n, pref, mult):
    if n <= pref:
        return n
    best = None
    for t in range(mult, pref + 1, mult):
        if n % t == 0:
            best = t
    assert best is not None, (n, pref, mult)
    return best


def _params(*sem):
    return pltpu.CompilerParams(dimension_semantics=sem, vmem_limit_bytes=VMEM_LIMIT)


def _dot(a, b):
    return jnp.dot(a, b, preferred_element_type=F32)


def _rmsnorm_kernel(x_ref, g_ref, o_ref):
    x = x_ref[...]
    y = x * lax.rsqrt(jnp.mean(x * x, axis=-1, keepdims=True) + EPS)
    o_ref[...] = (y * g_ref[...]).astype(o_ref.dtype)


def _rmsnorm(x, g):
    t, d = x.shape
    tr = _tile(t, 256, 16)
    return pl.pallas_call(
        _rmsnorm_kernel,
        out_shape=jax.ShapeDtypeStruct((t, d), BF16),
        grid=(t // tr,),
        in_specs=[pl.BlockSpec((tr, d), lambda i: (i, 0)),
                  pl.BlockSpec((1, d), lambda i: (0, 0))],
        out_specs=pl.BlockSpec((tr, d), lambda i: (i, 0)),
        compiler_params=_params("parallel"),
        name="rmsnorm",
    )(x, g.reshape(1, d))


def _mm_kernel(a_ref, b_ref, o_ref):
    o_ref[...] = _dot(a_ref[...], b_ref[...]).astype(o_ref.dtype)


def _matmul(a, b, col_off, width, out_dtype, tn_pref=1024):
    m, k = a.shape
    tm = _tile(m, ROW_TILE, 16)
    tn = _tile(math.gcd(width, col_off) if col_off else width, tn_pref, LANES)
    off = col_off // tn
    return pl.pallas_call(
        _mm_kernel,
        out_shape=jax.ShapeDtypeStruct((m, width), out_dtype),
        grid=(m // tm, width // tn),
        in_specs=[pl.BlockSpec((tm, k), lambda i, j: (i, 0)),
                  pl.BlockSpec((k, tn), lambda i, j: (0, j + off))],
        out_specs=pl.BlockSpec((tm, tn), lambda i, j: (i, j)),
        compiler_params=_params("parallel", "parallel"),
        name="in_proj",
    )(a, b)


def _proj_norm_kernel(a_ref, b_ref, g_ref, gm_ref, o_ref, *, group, scale):
    acc = _dot(a_ref[...], b_ref[...])
    gm = gm_ref[...]
    w = gm.shape[0]
    for c in range(0, acc.shape[1], w):
        x = acc[:, c:c + w]
        sq = x * x
        hi = sq.astype(BF16)
        lo = (sq - hi.astype(F32)).astype(BF16)
        ssq = _dot(hi, gm) + _dot(lo, gm)
        y = x * lax.rsqrt(ssq * (1.0 / group) + EPS)
        o_ref[:, c:c + w] = ((y * g_ref[...]) * scale).astype(o_ref.dtype)


def _proj_norm(a, b, col_off, width, gain, group, scale, out_dtype):
    m, k = a.shape
    gw = MXU_DIM
    tm = _tile(m, ROW_TILE, 16)
    tn = _tile(math.gcd(width, col_off) if col_off else width, 512, gw)
    assert gw % group == 0
    ids = jnp.arange(gw) // group
    gm = (ids[:, None] == ids[None, :]).astype(BF16)
    g = jnp.tile(gain.astype(F32), gw // group).reshape(1, gw)
    off = col_off // tn
    return pl.pallas_call(
        functools.partial(_proj_norm_kernel, group=group, scale=scale),
        out_shape=jax.ShapeDtypeStruct((m, width), out_dtype),
        grid=(m // tm, width // tn),
        in_specs=[pl.BlockSpec((tm, k), lambda i, j: (i, 0)),
                  pl.BlockSpec((k, tn), lambda i, j: (0, j + off)),
                  pl.BlockSpec((1, gw), lambda i, j: (0, 0)),
                  pl.BlockSpec((gw, gw), lambda i, j: (0, 0))],
        out_specs=pl.BlockSpec((tm, tn), lambda i, j: (i, j)),
        compiler_params=_params("parallel", "parallel"),
        name="qk_proj_norm",
    )(a, b, g, gm)


def _diff_lambda(lq1, lk1, lq2, lk2, lam_init):
    a = jnp.exp(jnp.sum(lq1[...] * lk1[...], axis=-1, keepdims=True))
    b = jnp.exp(jnp.sum(lq2[...] * lk2[...], axis=-1, keepdims=True))
    return a - b + lam_init


def _finish_heads(a1, l1, a2, l2, lam, sub, lam_init):
    o = a1 / l1 - lam * (a2 / l2)
    y = o * lax.rsqrt(jnp.mean(o * o, axis=-1, keepdims=True) + EPS)
    return (y * sub) * (1.0 - lam_init)


def _split_maps(q, dk):
    lane = lax.broadcasted_iota(jnp.int32, q.shape, 1)
    zero = jnp.zeros_like(q)
    return jnp.where(lane < dk, q, zero), jnp.where(lane >= dk, q, zero)


def _qk(q, k):
    return lax.dot_general(q, k, (((1,), (1,)), ((), ())), preferred_element_type=F32)


def _attn_prompt_kernel(q_ref, k_ref, v_ref, lq1, lk1, lq2, lk2, sub_ref, o_ref,
                        kb, vt, qt, m_s, l_s, acc, s_ref, p_ref, *, tq, dk, lam_init):
    qi = pl.program_id(2)
    hb, nk, tk = kb.shape[0], kb.shape[1], kb.shape[2]
    hw = 2 * dk
    kpq = tq // tk

    @pl.when(qi == 0)
    def _():
        for hh in range(hb):
            for jj in range(nk):
                kv = (slice(jj * tk, (jj + 1) * tk), slice(hh * hw, (hh + 1) * hw))
                kb[hh, jj] = k_ref[kv].astype(BF16)
                vt[hh, jj] = v_ref[kv].T.astype(BF16)

    for hh in range(hb):
        q = q_ref[:, hh * hw:(hh + 1) * hw].astype(F32)
        lane = lax.broadcasted_iota(jnp.int32, q.shape, 1)
        qt[hh, :, 0:tq] = jnp.where(lane < dk, q, 0.0).T.astype(BF16)
        qt[hh, :, tq:2 * tq] = jnp.where(lane >= dk, q, 0.0).T.astype(BF16)
    m_s[...] = jnp.full(m_s.shape, NEG_INF, F32)
    l_s[...] = jnp.zeros(l_s.shape, F32)
    acc[...] = jnp.zeros(acc.shape, F32)

    def block(j, key0):
        for hh in range(hb):
            s_ref[hh] = _dot(kb[hh, j], qt[hh])
        for hh in range(hb):
            for c in range(0, 2 * tq, LANES):
                if key0 is not None and (c % tq) + LANES <= key0:
                    p_ref[hh, :, c:c + LANES] = jnp.zeros((tk, LANES), BF16)
                    continue
                s = s_ref[hh, :, c:c + LANES]
                if key0 is not None:
                    key = lax.broadcasted_iota(jnp.int32, s.shape, 0) + key0
                    qry = lax.broadcasted_iota(jnp.int32, s.shape, 1) + (c % tq)
                    s = jnp.where(key // CHUNK <= qry // CHUNK, s, NEG_INF)
                m_old = m_s[hh, :, c:c + LANES]
                m_new = jnp.maximum(m_old, jnp.max(s, axis=0, keepdims=True))
                alpha = jnp.exp(m_old - m_new)
                p = jnp.exp(s - m_new)
                l_s[hh, :, c:c + LANES] = (alpha * l_s[hh, :, c:c + LANES]
                                           + jnp.sum(p, axis=0, keepdims=True))
                m_s[hh, :, c:c + LANES] = m_new
                p_ref[hh, :, c:c + LANES] = p.astype(BF16)
                acc[hh, :, c:c + LANES] = alpha * acc[hh, :, c:c + LANES]
            acc[hh] += _dot(vt[hh, j], p_ref[hh])

    def body(j, c):
        block(j, None)
        return c

    lax.fori_loop(0, qi * kpq, body, 0)
    for d in range(kpq):
        block(qi * kpq + d, d * tk)
    lam = _diff_lambda(lq1, lk1, lq2, lk2, lam_init)
    for hh in range(hb):
        r = acc[hh] / l_s[hh]
        o = (r[:, 0:tq] - lam * r[:, tq:2 * tq]).T
        y = o * lax.rsqrt(jnp.mean(o * o, axis=-1, keepdims=True) + EPS)
        o_ref[:, hh * hw:(hh + 1) * hw] = ((y * sub_ref[...])
                                           * (1.0 - lam_init)).astype(o_ref.dtype)


def _attn_prompt(q, kv_src, k_col, v_src, v_col, nb, seq, heads, dk, lam_rows, sub, lam_init):
    hw = 2 * dk
    tq = _tile(seq, ATTN_BLOCK, LANES)
    tk = _tile(tq, ATTN_KEY_BLOCK, CHUNK)
    nq, nk = seq // tq, seq // tk
    hb = next(n for n in (ATTN_HEADS, 1) if heads % n == 0)
    bw = hb * hw
    assert (k_col * hw) % bw == 0 and (v_col * hw) % bw == 0
    kc, vc = k_col * hw // bw, v_col * hw // bw
    vec = lambda i, h, j: (0, 0)
    return pl.pallas_call(
        functools.partial(_attn_prompt_kernel, tq=tq, dk=dk, lam_init=lam_init),
        out_shape=jax.ShapeDtypeStruct((nb * seq, heads * hw), BF16),
        grid=(nb, heads // hb, nq),
        in_specs=[pl.BlockSpec((tq, bw), lambda b, h, j: (b * nq + j, h)),
                  pl.BlockSpec((seq, bw), lambda b, h, j: (b, kc + h)),
                  pl.BlockSpec((seq, bw), lambda b, h, j: (b, vc + h)),
                  pl.BlockSpec((1, dk), vec), pl.BlockSpec((1, dk), vec),
                  pl.BlockSpec((1, dk), vec), pl.BlockSpec((1, dk), vec),
                  pl.BlockSpec((1, hw), vec)],
        out_specs=pl.BlockSpec((tq, bw), lambda b, h, j: (b * nq + j, h)),
        scratch_shapes=[pltpu.VMEM((hb, nk, tk, hw), BF16), pltpu.VMEM((hb, nk, hw, tk), BF16),
                        pltpu.VMEM((hb, hw, 2 * tq), BF16), pltpu.VMEM((hb, 1, 2 * tq), F32),
                        pltpu.VMEM((hb, 1, 2 * tq), F32), pltpu.VMEM((hb, hw, 2 * tq), F32),
                        pltpu.VMEM((hb, tk, 2 * tq), F32), pltpu.VMEM((hb, tk, 2 * tq), BF16)],
        compiler_params=_params("parallel", "parallel", "arbitrary"),
        name="attn_prompt",
    )(q, kv_src, v_src, *lam_rows, sub)


def _attn_sample_kernel(q_ref, kn_ref, vn_ref, kp_ref, vp_ref, lq1, lk1, lq2, lk2, sub_ref, o_ref,
                        *, dk, lam_init):
    hw = 2 * dk
    lam = _diff_lambda(lq1, lk1, lq2, lk2, lam_init)
    for c in range(0, q_ref.shape[1], hw):
        q1, q2 = _split_maps(q_ref[:, c:c + hw], dk)
        kp = kp_ref[:, c:c + hw].astype(BF16)
        vp = vp_ref[:, c:c + hw].astype(BF16)
        kn = kn_ref[:, c:c + hw].astype(BF16)
        vn = vn_ref[:, c:c + hw].astype(BF16)

        def one_map(q):
            sp = _qk(q, kp)
            sn = _qk(q, kn)
            m = jnp.maximum(jnp.max(sp, axis=-1, keepdims=True),
                            jnp.max(sn, axis=-1, keepdims=True))
            ep = jnp.exp(sp - m)
            en = jnp.exp(sn - m)
            l = jnp.sum(ep, axis=-1, keepdims=True) + jnp.sum(en, axis=-1, keepdims=True)
            a = _dot(ep.astype(BF16), vp) + _dot(en.astype(BF16), vn)
            return l, a

        l1, a1 = one_map(q1)
        l2, a2 = one_map(q2)
        o_ref[:, c:c + hw] = _finish_heads(a1, l1, a2, l2, lam, sub_ref[...],
                                           lam_init).astype(o_ref.dtype)


def _attn_sample(q, k_new, v_src, v_col, past_k, past_v, layer, row_off, nb, seq, heads, dk,
                 lam_rows, sub, lam_init):
    hw = 2 * dk
    past = past_k.shape[2]
    rb = row_off // seq
    hb = next(n for n in (4, 2, 1) if heads % n == 0)
    bw = hb * hw
    assert (v_col * hw) % bw == 0
    vc = v_col * hw // bw
    vec = lambda b, h: (0, 0)
    return pl.pallas_call(
        functools.partial(_attn_sample_kernel, dk=dk, lam_init=lam_init),
        out_shape=jax.ShapeDtypeStruct((nb * seq, heads * hw), BF16),
        grid=(nb, heads // hb),
        in_specs=[pl.BlockSpec((seq, bw), lambda b, h: (rb + b, h)),
                  pl.BlockSpec((seq, bw), lambda b, h: (rb + b, h)),
                  pl.BlockSpec((seq, bw), lambda b, h: (rb + b, vc + h)),
                  pl.BlockSpec((None, None, past, bw), lambda b, h: (layer, b, 0, h)),
                  pl.BlockSpec((None, None, past, bw), lambda b, h: (layer, b, 0, h)),
                  pl.BlockSpec((1, dk), vec), pl.BlockSpec((1, dk), vec),
                  pl.BlockSpec((1, dk), vec), pl.BlockSpec((1, dk), vec),
                  pl.BlockSpec((1, hw), vec)],
        out_specs=pl.BlockSpec((seq, bw), lambda b, h: (b, h)),
        compiler_params=_params("parallel", "parallel"),
        name="attn_sample",
    )(q, k_new, v_src, past_k, past_v, *lam_rows, sub)


def _conv_kernel(bg_ref, cg_ref, hc_ref, st_ref, cw_ref, y_ref, nc_ref, carry, *, ts):
    s = pl.program_id(2)
    u = cg_ref[...] * hc_ref[...]

    @pl.when(s == 0)
    def _():
        carry[6:8, :] = st_ref[...]

    prev = carry[...]
    p2 = prev[6:7, :]
    p1 = prev[7:8, :]
    row = lax.broadcasted_iota(jnp.int32, u.shape, 0)
    u1 = jnp.where(row == 0, p1, pltpu.roll(u, 1, 0))
    u2 = jnp.where(row == 0, p2, jnp.where(row == 1, p1, pltpu.roll(u, 2, 0)))
    cw = cw_ref[...]
    y = bg_ref[...] * (cw[0:1, :] * u2 + cw[1:2, :] * u1 + cw[2:3, :] * u)
    y_ref[...] = y.astype(y_ref.dtype)
    carry[...] = u[ts - 8:ts, :]

    @pl.when(s == pl.num_programs(2) - 1)
    def _():
        nc_ref[...] = u[ts - 2:ts, :]


def _conv(p, col_b, col_c, col_h, width, row_off, nb, seq, state, cw):
    assert seq >= 8 and state.shape[1] == 2 and cw.shape[0] == 3
    ts = _tile(seq, 512, 8)
    tc = _tile(width, 512, LANES)
    ns = seq // ts
    rb = row_off // ts
    ob, oc, oh = col_b // tc, col_c // tc, col_h // tc
    blk = lambda o: pl.BlockSpec((ts, tc), lambda b, c, s: (rb + b * ns + s, o + c))
    return pl.pallas_call(
        functools.partial(_conv_kernel, ts=ts),
        out_shape=(jax.ShapeDtypeStruct((nb * seq, width), BF16),
                   jax.ShapeDtypeStruct((nb, 2, width), F32)),
        grid=(nb, width // tc, ns),
        in_specs=[blk(ob), blk(oc), blk(oh),
                  pl.BlockSpec((None, 2, tc), lambda b, c, s: (b, 0, c)),
                  pl.BlockSpec((3, tc), lambda b, c, s: (0, c))],
        out_specs=(pl.BlockSpec((ts, tc), lambda b, c, s: (b * ns + s, c)),
                   pl.BlockSpec((None, 2, tc), lambda b, c, s: (b, 0, c))),
        scratch_shapes=[pltpu.VMEM((8, tc), F32)],
        compiler_params=_params("parallel", "parallel", "arbitrary"),
        name="short_conv",
    )(p, p, p, state, cw)


def _outproj_kernel(a1_ref, a2_ref, b1_ref, b2_ref, x_ref, o_ref):
    o_ref[...] = x_ref[...] + (_dot(a1_ref[...], b1_ref[...]) + _dot(a2_ref[...], b2_ref[...]))


def _outproj(a1, a2, w, x):
    m, k1 = a1.shape
    k2 = a2.shape[1]
    assert k1 == k2 and w.shape[0] == k1 + k2
    n = w.shape[1]
    tm = _tile(m, ROW_TILE, 16)
    tn = _tile(n, 512, LANES)
    return pl.pallas_call(
        _outproj_kernel,
        out_shape=jax.ShapeDtypeStruct((m, n), F32),
        grid=(m // tm, n // tn),
        in_specs=[pl.BlockSpec((tm, k1), lambda i, j: (i, 0)),
                  pl.BlockSpec((tm, k2), lambda i, j: (i, 0)),
                  pl.BlockSpec((k1, tn), lambda i, j: (0, j)),
                  pl.BlockSpec((k2, tn), lambda i, j: (1, j)),
                  pl.BlockSpec((tm, tn), lambda i, j: (i, j))],
        out_specs=pl.BlockSpec((tm, tn), lambda i, j: (i, j)),
        compiler_params=_params("parallel", "parallel"),
        name="out_proj",
    )(a1, a2, w, w, x)


def _silu_mul(g, u):
    return (g / (1.0 + jnp.exp(-g))) * u


def _gateup_kernel(a_ref, w1_ref, w3_ref, o_ref):
    a = a_ref[...]
    o_ref[...] = _silu_mul(_dot(a, w1_ref[...]), _dot(a, w3_ref[...])).astype(o_ref.dtype)


def _gateup_cast_kernel(a_ref, w1_ref, w3_ref, c_ref, o_ref, co_ref):
    _gateup_kernel(a_ref, w1_ref, w3_ref, o_ref)
    co_ref[...] = c_ref[...].astype(co_ref.dtype)


def _gateup(a, w1, w3, cast_src=None, cast_layer=0):
    m, k = a.shape
    n = w1.shape[1]
    tm = _tile(m, ROW_TILE, 16)
    tn = _tile(n, 512, LANES)
    nj = n // tn
    steps = (m // tm) * nj
    in_specs = [pl.BlockSpec((tm, k), lambda i, j: (i, 0)),
                pl.BlockSpec((k, tn), lambda i, j: (0, j)),
                pl.BlockSpec((k, tn), lambda i, j: (0, j))]
    out_spec = pl.BlockSpec((tm, tn), lambda i, j: (i, j))
    out_shape = jax.ShapeDtypeStruct((m, n), BF16)
    if cast_src is None:
        return pl.pallas_call(
            _gateup_kernel, out_shape=out_shape, grid=(m // tm, nj), in_specs=in_specs,
            out_specs=out_spec, compiler_params=_params("parallel", "parallel"),
            name="ffn_gate_up",
        )(a, w1, w3), None
    _, cr, cc = cast_src.shape
    assert cr % steps == 0 and (cr // steps) % 16 == 0, "cast rows must split evenly over steps"
    cb = cr // steps
    return pl.pallas_call(
        _gateup_cast_kernel,
        out_shape=(out_shape, jax.ShapeDtypeStruct((cr, cc), BF16)),
        grid=(m // tm, nj),
        in_specs=in_specs + [pl.BlockSpec((None, cb, cc),
                                          lambda i, j: (cast_layer, i * nj + j, 0))],
        out_specs=(out_spec, pl.BlockSpec((cb, cc), lambda i, j: (i * nj + j, 0))),
        compiler_params=_params("parallel", "parallel"),
        name="ffn_gate_up",
    )(a, w1, w3, cast_src)


def _down_kernel(a_ref, b_ref, x_ref, o_ref):
    d = _dot(a_ref[...], b_ref[...])

    @pl.when(pl.program_id(2) == 0)
    def _():
        o_ref[...] = x_ref[...] + d

    @pl.when(pl.program_id(2) > 0)
    def _():
        o_ref[...] += d


def _down(a, w, x):
    m, k = a.shape
    n = w.shape[1]
    tm = _tile(m, ROW_TILE, 16)
    tn = _tile(n, 1024, LANES)
    tk = _tile(k, 2048, MXU_DIM)
    return pl.pallas_call(
        _down_kernel,
        out_shape=jax.ShapeDtypeStruct((m, n), F32),
        grid=(m // tm, n // tn, k // tk),
        in_specs=[pl.BlockSpec((tm, tk), lambda i, j, kk: (i, kk)),
                  pl.BlockSpec((tk, tn), lambda i, j, kk: (kk, j)),
                  pl.BlockSpec((tm, tn), lambda i, j, kk: (i, j))],
        out_specs=pl.BlockSpec((tm, tn), lambda i, j, kk: (i, j)),
        compiler_params=_params("parallel", "parallel", "arbitrary"),
        name="ffn_down",
    )(a, w, x)


def _grouped_gateup_kernel(te_ref, na_ref, rows_ref, a_ref, w1_ref, w3_ref, o_ref, *, sub):
    rows = rows_ref[pl.program_id(0)]
    tm = a_ref.shape[0]

    def compute(lo, size):
        a = a_ref[lo:lo + size, :]
        g = _dot(a, w1_ref[...].astype(BF16))
        u = _dot(a, w3_ref[...].astype(BF16))
        o_ref[lo:lo + size, :] = _silu_mul(g, u).astype(o_ref.dtype)

    @pl.when(rows == tm)
    def _():
        compute(0, tm)

    for s in range(tm // sub):
        @pl.when((rows < tm) & (s * sub < rows))
        def _():
            compute(s * sub, sub)

        @pl.when((rows < tm) & (s * sub >= rows))
        def _():
            o_ref[s * sub:(s + 1) * sub, :] = jnp.zeros((sub, o_ref.shape[1]), o_ref.dtype)


def _grouped_gateup(xg, w1, w3, layer, tile_expert, n_active, tile_rows, tm):
    r, k = xg.shape
    n = w1.shape[3]
    tn = _tile(n, 256, LANES)
    nj = n // tn
    sub = _tile(tm, EXPERT_SUB_ROWS, 16)
    wspec = pl.BlockSpec(
        (None, None, k, tn),
        lambda i, j, te, na, tr: (layer, te[i], 0, jnp.where(i < na[0], j, nj - 1)))
    return pl.pallas_call(
        functools.partial(_grouped_gateup_kernel, sub=sub),
        out_shape=jax.ShapeDtypeStruct((r, n), BF16),
        grid_spec=pltpu.PrefetchScalarGridSpec(
            num_scalar_prefetch=3,
            grid=(r // tm, nj),
            in_specs=[pl.BlockSpec((tm, k),
                                   lambda i, j, te, na, tr: (jnp.minimum(i, na[0] - 1), 0)),
                      wspec, wspec],
            out_specs=pl.BlockSpec((tm, tn), lambda i, j, te, na, tr: (i, j))),
        compiler_params=_params("parallel", "parallel"),
        name="moe_gate_up",
    )(tile_expert, n_active, tile_rows, xg, w1, w3)


def _grouped_down_kernel(te_ref, na_ref, rows_ref, a_ref, b_ref, o_ref, *, sub):
    rows = rows_ref[pl.program_id(0)]
    first = pl.program_id(2) == 0
    tm = a_ref.shape[0]

    def compute(lo, size):
        d = _dot(a_ref[lo:lo + size, :], b_ref[...])

        @pl.when(first)
        def _():
            o_ref[lo:lo + size, :] = d

        @pl.when(jnp.logical_not(first))
        def _():
            o_ref[lo:lo + size, :] += d

    @pl.when(rows == tm)
    def _():
        compute(0, tm)

    for s in range(tm // sub):
        @pl.when((rows < tm) & (s * sub < rows))
        def _():
            compute(s * sub, sub)

        @pl.when((rows < tm) & (s * sub >= rows) & first)
        def _():
            o_ref[s * sub:(s + 1) * sub, :] = jnp.zeros((sub, o_ref.shape[1]), o_ref.dtype)


def _grouped_down(g, w, tile_expert, n_active, tile_rows, tm):
    r, k = g.shape
    n = w.shape[2]
    tn = _tile(n, 1024, LANES)
    tk = _tile(k, 2048, MXU_DIM)
    nj, nk = n // tn, k // tk
    sub = _tile(tm, EXPERT_SUB_ROWS, 16)

    def a_map(i, j, kk, te, na, tr):
        act = i < na[0]
        return jnp.minimum(i, na[0] - 1), jnp.where(act, kk, nk - 1)

    def b_map(i, j, kk, te, na, tr):
        act = i < na[0]
        return te[i], jnp.where(act, kk, nk - 1), jnp.where(act, j, nj - 1)

    return pl.pallas_call(
        functools.partial(_grouped_down_kernel, sub=sub),
        out_shape=jax.ShapeDtypeStruct((r, n), F32),
        grid_spec=pltpu.PrefetchScalarGridSpec(
            num_scalar_prefetch=3,
            grid=(r // tm, nj, nk),
            in_specs=[pl.BlockSpec((tm, tk), a_map),
                      pl.BlockSpec((None, tk, tn), b_map)],
            out_specs=pl.BlockSpec((tm, tn), lambda i, j, kk, te, na, tr: (i, j))),
        compiler_params=_params("parallel", "parallel", "arbitrary"),
        name="moe_down",
    )(tile_expert, n_active, tile_rows, g, w)


def _router_kernel(x_ref, g_ref, wr_ref, h_ref, r_ref, *, n_exp):
    x = x_ref[...]
    h = (x * lax.rsqrt(jnp.mean(x * x, axis=-1, keepdims=True) + EPS)) * g_ref[...]
    h_ref[...] = h
    logits = jnp.dot(h, wr_ref[...], precision=lax.Precision.HIGHEST, preferred_element_type=F32)
    lane = lax.broadcasted_iota(jnp.int32, logits.shape, 1)
    lanef = lane.astype(F32)
    lg = jnp.where(lane < n_exp, logits, -jnp.inf)
    m1 = jnp.max(lg, axis=-1, keepdims=True)
    i1 = jnp.min(jnp.where(lg == m1, lanef, float(LANES)), axis=-1, keepdims=True)
    lg2 = jnp.where(lanef == i1, -jnp.inf, lg)
    m2 = jnp.max(lg2, axis=-1, keepdims=True)
    i2 = jnp.min(jnp.where(lg2 == m2, lanef, float(LANES)), axis=-1, keepdims=True)
    e = jnp.exp(m2 - m1)
    den = 1.0 + e
    r_ref[...] = jnp.where(lane == 0, i1,
                 jnp.where(lane == 1, i2,
                 jnp.where(lane == 2, 1.0 / den,
                 jnp.where(lane == 3, e / den, 0.0))))


def _router(x, g, w_router):
    t, d = x.shape
    n_exp = w_router.shape[1]
    assert n_exp <= LANES and TOP_K == 2
    wr = jnp.zeros((d, LANES), F32).at[:, :n_exp].set(w_router.astype(F32))
    tr = _tile(t, 256, 8)
    return pl.pallas_call(
        functools.partial(_router_kernel, n_exp=n_exp),
        out_shape=(jax.ShapeDtypeStruct((t, d), F32), jax.ShapeDtypeStruct((t, LANES), F32)),
        grid=(t // tr,),
        in_specs=[pl.BlockSpec((tr, d), lambda i: (i, 0)),
                  pl.BlockSpec((1, d), lambda i: (0, 0)),
                  pl.BlockSpec((d, LANES), lambda i: (0, 0))],
        out_specs=(pl.BlockSpec((tr, d), lambda i: (i, 0)),
                   pl.BlockSpec((tr, LANES), lambda i: (i, 0))),
        compiler_params=_params("parallel"),
        name="ffn_norm_router",
    )(x, g.reshape(1, d), wr)


def _row_copy(src_hbm, idx, dst_vmem, row, sem):
    return pltpu.make_async_copy(src_hbm.at[pl.ds(idx, 1), :], dst_vmem.at[pl.ds(row, 1), :], sem)


def _gather_kernel(na_ref, src_ref, nxt_ref, h_hbm, o_ref, buf, sem, *, rows):
    i = pl.program_id(0)
    slot = i % 2

    def issue(idx_ref, sl):
        def body(r, c):
            _row_copy(h_hbm, idx_ref[0, 0, r], buf.at[sl], r, sem.at[sl]).start()
            return c
        lax.fori_loop(0, rows, body, 0, unroll=8)

    @pl.when(i == 0)
    def _():
        issue(src_ref, 0)

    @pl.when(i + 1 < na_ref[0])
    def _():
        issue(nxt_ref, 1 - slot)

    @pl.when(i < na_ref[0])
    def _():
        def wait(r, c):
            _row_copy(h_hbm, src_ref[0, 0, r], buf.at[slot], r, sem.at[slot]).wait()
            return c
        lax.fori_loop(0, rows, wait, 0, unroll=8)
        o_ref[...] = buf[slot].astype(o_ref.dtype)

    @pl.when(i >= na_ref[0])
    def _():
        o_ref[...] = jnp.zeros_like(o_ref)


def _gather_rows(h, src, n_active_steps, rows):
    r = src.shape[0]
    d = h.shape[1]
    steps = r // rows
    src3 = src.reshape(steps, 1, rows)
    idx_spec = lambda f: pl.BlockSpec((1, 1, rows), f, memory_space=pltpu.SMEM)
    return pl.pallas_call(
        functools.partial(_gather_kernel, rows=rows),
        out_shape=jax.ShapeDtypeStruct((r, d), BF16),
        grid_spec=pltpu.PrefetchScalarGridSpec(
            num_scalar_prefetch=1,
            grid=(steps,),
            in_specs=[idx_spec(lambda i, na: (i, 0, 0)),
                      idx_spec(lambda i, na: (jnp.minimum(i + 1, steps - 1), 0, 0)),
                      pl.BlockSpec(memory_space=pl.ANY)],
            out_specs=pl.BlockSpec((rows, d), lambda i, na: (i, 0)),
            scratch_shapes=[pltpu.VMEM((2, rows, d), F32), pltpu.SemaphoreType.DMA((2,))]),
        compiler_params=_params("arbitrary"),
        name="moe_gather",
    )(n_active_steps, src3, src3, h)


def _combine_kernel(dst_ref, eo_hbm, x_ref, r_ref, o_ref, buf, sem, *, rows):
    def start(r, c):
        _row_copy(eo_hbm, dst_ref[0, 0, r], buf.at[0], r, sem.at[0]).start()
        _row_copy(eo_hbm, dst_ref[0, 0, rows + r], buf.at[1], r, sem.at[0]).start()
        return c

    def wait(r, c):
        _row_copy(eo_hbm, dst_ref[0, 0, r], buf.at[0], r, sem.at[0]).wait()
        _row_copy(eo_hbm, dst_ref[0, 0, rows + r], buf.at[1], r, sem.at[0]).wait()
        return c

    lax.fori_loop(0, rows, start, 0, unroll=8)
    lax.fori_loop(0, rows, wait, 0, unroll=8)
    route = r_ref[...]
    o_ref[...] = x_ref[...] + (route[:, 2:3] * buf[0] + route[:, 3:4] * buf[1])


def _combine(eo, dst, x, route, rows):
    t, d = x.shape
    steps = t // rows
    return pl.pallas_call(
        functools.partial(_combine_kernel, rows=rows),
        out_shape=jax.ShapeDtypeStruct((t, d), F32),
        grid=(steps,),
        in_specs=[pl.BlockSpec((1, 1, 2 * rows), lambda i: (i, 0, 0), memory_space=pltpu.SMEM),
                  pl.BlockSpec(memory_space=pl.ANY),
                  pl.BlockSpec((rows, d), lambda i: (i, 0)),
                  pl.BlockSpec((rows, LANES), lambda i: (i, 0))],
        out_specs=pl.BlockSpec((rows, d), lambda i: (i, 0)),
        scratch_shapes=[pltpu.VMEM((2, rows, d), F32), pltpu.SemaphoreType.DMA((1,))],
        compiler_params=_params("arbitrary"),
        name="moe_combine",
    )(dst, eo, x, route)


def _moe_ffn(x, g, w_router, w1, w3, layer, w2):
    t, d = x.shape
    n_exp = w_router.shape[1]
    tm = EXPERT_ROW_TILE
    rows_g = _tile(tm, GATHER_ROWS, 8)
    rows_c = _tile(t, GATHER_ROWS, 8)
    n_tiles = -(-(TOP_K * t) // tm) + n_exp
    r = n_tiles * tm

    h, route = _router(x, g, w_router)

    e_flat = jnp.concatenate([route[:, 0], route[:, 1]]).astype(jnp.int32)
    onehot = (e_flat[:, None] == jnp.arange(n_exp, dtype=jnp.int32)[None, :]).astype(jnp.int32)
    rank = jnp.sum((jnp.cumsum(onehot, axis=0) - onehot) * onehot, axis=1)
    counts = jnp.sum(onehot, axis=0)
    tiles_per = (counts + tm - 1) // tm
    tile_end = jnp.cumsum(tiles_per)
    tile_start = tile_end - tiles_per
    n_active = tile_end[-1]
    dst = tile_start[e_flat] * tm + rank
    tile_ids = jnp.arange(n_tiles, dtype=jnp.int32)
    tile_expert = jnp.minimum(jnp.searchsorted(tile_end, tile_ids, side="right"), n_exp - 1)
    last_expert = tile_expert[jnp.maximum(n_active - 1, 0)]
    tile_rows = jnp.clip(counts[tile_expert] - (tile_ids - tile_start[tile_expert]) * tm, 0, tm)
    tile_rows = jnp.where(tile_ids < n_active, tile_rows, 0).astype(jnp.int32)
    tile_expert = jnp.where(tile_ids < n_active, tile_expert, last_expert).astype(jnp.int32)
    order = jnp.argsort(e_flat, stable=True).astype(jnp.int32)
    group_off = jnp.cumsum(counts) - counts
    slot = jnp.arange(r, dtype=jnp.int32)
    slot_e = tile_expert[slot // tm]
    within = slot - tile_start[slot_e] * tm
    valid = (within < counts[slot_e]) & (slot // tm < n_active)
    src = jnp.where(valid, order[jnp.clip(group_off[slot_e] + within, 0, TOP_K * t - 1)] % t, 0)
    src = src.astype(jnp.int32)

    na = n_active.astype(jnp.int32).reshape(1)
    xg = _gather_rows(h, src, na * (tm // rows_g), rows_g)
    gu = _grouped_gateup(xg, w1, w3, layer, tile_expert, na, tile_rows, tm)
    eo = _grouped_down(gu, w2, tile_expert, na, tile_rows, tm)
    steps = t // rows_c
    dst2 = jnp.concatenate([dst[:t].reshape(steps, 1, rows_c), dst[t:].reshape(steps, 1, rows_c)],
                           axis=2).astype(jnp.int32)
    return _combine(eo, dst2, x, route, rows_c)


def kernel(x_prompt, x_sample, cache_k, cache_v, cache_conv, w_in, w_out, conv_w, attn_norm,
           q_norm, k_norm, lambda_q1, lambda_k1, lambda_q2, lambda_k2, subln, ffn_norm,
           w1_dense, w3_dense, w2_dense, w_router, w1_exp, w3_exp, w2_exp):
    depth = w_in.shape[0]
    nbp, seq, d = x_prompt.shape
    nbs, sseq, _ = x_sample.shape
    past, heads, _, dk = cache_k.shape[2:]
    dv = cache_v.shape[-1]
    cdim = cache_conv.shape[-1]
    hw = 2 * dk
    assert dv == hw == LANES, "one head (both maps / the value row) must span one lane tile"
    qk_dim = heads * hw
    attn_dim = heads * dv
    tp, ts = nbp * seq, nbs * sseq
    assert tp % sseq == 0 and seq % CHUNK == 0
    c_q, c_k, c_v = 0, qk_dim, 2 * qk_dim
    c_b = c_v + attn_dim

    x = jnp.concatenate([x_prompt.reshape(tp, d), x_sample.reshape(ts, d)], axis=0)
    zero_state = jnp.zeros((nbp, 2, cdim), F32)
    ck = cache_k.reshape(depth, nbs, past, qk_dim)
    cv = cache_v.reshape(depth, nbs, past, attn_dim)

    ks, vs, convs_p, convs_s = [], [], [], []
    for l in range(depth):
        lam0 = _lambda_init(l)
        lam_rows = [a[l].reshape(1, dk) for a in (lambda_q1, lambda_k1, lambda_q2, lambda_k2)]
        sub = subln[l].reshape(1, dv)

        h = _rmsnorm(x, attn_norm[l])
        wi = w_in[l].astype(BF16)
        q = _proj_norm(h, wi, c_q, qk_dim, q_norm[l], dk, dk ** -0.5, BF16)
        k = _proj_norm(h, wi, c_k, qk_dim, k_norm[l], dk, 1.0, F32)
        v = _matmul(h, wi, c_v, attn_dim, F32)
        pc = _matmul(h, wi, c_b, 3 * cdim, F32)
        o_p = _attn_prompt(q, k, 0, v, 0, nbp, seq, heads, dk, lam_rows, sub, lam0)
        o_s = _attn_sample(q, k, v, 0, ck, cv, l, tp, nbs, sseq, heads, dk, lam_rows, sub, lam0)
        y_p, nc_p = _conv(pc, 0, cdim, 2 * cdim, cdim, 0, nbp, seq, zero_state, conv_w[l])
        y_s, nc_s = _conv(pc, 0, cdim, 2 * cdim, cdim, tp, nbs, sseq, cache_conv[l], conv_w[l])
        x = _outproj(jnp.concatenate([o_p, o_s], axis=0), jnp.concatenate([y_p, y_s], axis=0),
                     w_out[l].astype(BF16), x)
        ks.append(k)
        vs.append(v)
        convs_p.append(nc_p)
        convs_s.append(nc_s)

        m = l // 2
        if l % 2 == 0:
            hf = _rmsnorm(x, ffn_norm[l])
            nxt = w2_exp.reshape(w2_exp.shape[0], -1, d) if l + 1 < depth else None
            gu, w2_next = _gateup(hf, w1_dense[m].astype(BF16), w3_dense[m].astype(BF16), nxt, m)
            x = _down(gu, w2_dense[m].astype(BF16), x)
        else:
            w2 = w2_next.reshape(w2_exp.shape[1:])
            x = _moe_ffn(x, ffn_norm[l], w_router[m], w1_exp, w3_exp, m, w2)

    k_all = jnp.stack(ks)
    v_all = jnp.stack(vs)
    return (x[:tp].reshape(nbp, seq, d),
            x[tp:].reshape(nbs, sseq, d),
            k_all[:, :tp].reshape(depth, nbp, seq, heads, 2, dk),
            v_all[:, :tp].reshape(depth, nbp, seq, heads, dv),
            jnp.stack(convs_p),
            k_all[:, tp:].reshape(depth, nbs, sseq, heads, 2, dk),
            v_all[:, tp:].reshape(depth, nbs, sseq, heads, dv),
            jnp.stack(convs_s))
```

```python
import functools
import math

import jax
import jax.numpy as jnp
from jax import lax
from jax.experimental import pallas as pl
from jax.experimental.pallas import tpu as pltpu

CHUNK = 64
TOP_K = 2
EPS = 1e-6
NEG_INF = -1e30
F32 = jnp.float32
BF16 = jnp.bfloat16

LANES = 128
MXU_DIM = 256
VMEM_LIMIT = 60000 * 1024

ROW_TILE = 1100
EXPERT_ROW_TILE = 1536
EXPERT_SUB_ROWS = 384
GATHER_ROWS = 192
COMBINE_ROWS = 256
ATTN_BLOCK = 512
ATTN_KEY_BLOCK = 256
ATTN_HEADS = 4


def _lambda_init(l):
    return 0.8 - 0.6 * math.exp(-0.3 * l)


def _tile(n, pref, mult):
    if n <= pref:
        return n
    best = None
    for t in range(mult, pref + 1, mult):
        if n % t == 0:
            best = t
    assert best is not None, (n, pref, mult)
    return best


def _params(*sem):
    return pltpu.CompilerParams(dimension_semantics=sem, vmem_limit_bytes=VMEM_LIMIT)


def _dot(a, b):
    return jnp.dot(a, b, preferred_element_type=F32)


def _rmsnorm_kernel(x_ref, g_ref, o_ref):
    x = x_ref[...]
    y = x * lax.rsqrt(jnp.mean(x * x, axis=-1, keepdims=True) + EPS)
    o_ref[...] = (y * g_ref[...]).astype(o_ref.dtype)


def _rmsnorm(x, g):
    t, d = x.shape
    tr = _tile(t, 256, 16)
    return pl.pallas_call(
        _rmsnorm_kernel,
        out_shape=jax.ShapeDtypeStruct((t, d), BF16),
        grid=(t // tr,),
        in_specs=[pl.BlockSpec((tr, d), lambda i: (i, 0)),
                  pl.BlockSpec((1, d), lambda i: (0, 0))],
        out_specs=pl.BlockSpec((tr, d), lambda i: (i, 0)),
        compiler_params=_params("parallel"),
        name="rmsnorm",
    )(x, g.reshape(1, d))


def _mm_kernel(a_ref, b_ref, o_ref):
    o_ref[...] = _dot(a_ref[...], b_ref[...]).astype(o_ref.dtype)


def _matmul(a, b, col_off, width, out_dtype, tn_pref=1024):
    m, k = a.shape
    tm = _tile(m, ROW_TILE, 16)
    tn = _tile(math.gcd(width, col_off) if col_off else width, tn_pref, LANES)
    off = col_off // tn
    return pl.pallas_call(
        _mm_kernel,
        out_shape=jax.ShapeDtypeStruct((m, width), out_dtype),
        grid=(m // tm, width // tn),
        in_specs=[pl.BlockSpec((tm, k), lambda i, j: (i, 0)),
                  pl.BlockSpec((k, tn), lambda i, j: (0, j + off))],
        out_specs=pl.BlockSpec((tm, tn), lambda i, j: (i, j)),
        compiler_params=_params("parallel", "parallel"),
        name="in_proj",
    )(a, b)


def _proj_norm_kernel(a_ref, b_ref, g_ref, gm_ref, o_ref, *, group, scale):
    acc = _dot(a_ref[...], b_ref[...])
    gm = gm_ref[...]
    w = gm.shape[0]
    for c in range(0, acc.shape[1], w):
        x = acc[:, c:c + w]
        sq = x * x
        hi = sq.astype(BF16)
        lo = (sq - hi.astype(F32)).astype(BF16)
        ssq = _dot(hi, gm) + _dot(lo, gm)
        y = x * lax.rsqrt(ssq * (1.0 / group) + EPS)
        o_ref[:, c:c + w] = ((y * g_ref[...]) * scale).astype(o_ref.dtype)


def _proj_norm(a, b, col_off, width, gain, group, scale, out_dtype):
    m, k = a.shape
    gw = MXU_DIM
    tm = _tile(m, ROW_TILE, 16)
    tn = _tile(math.gcd(width, col_off) if col_off else width, 512, gw)
    assert gw % group == 0
    ids = jnp.arange(gw) // group
    gm = (ids[:, None] == ids[None, :]).astype(BF16)
    g = jnp.tile(gain.astype(F32), gw // group).reshape(1, gw)
    off = col_off // tn
    return pl.pallas_call(
        functools.partial(_proj_norm_kernel, group=group, scale=scale),
        out_shape=jax.ShapeDtypeStruct((m, width), out_dtype),
        grid=(m // tm, width // tn),
        in_specs=[pl.BlockSpec((tm, k), lambda i, j: (i, 0)),
                  pl.BlockSpec((k, tn), lambda i, j: (0, j + off)),
                  pl.BlockSpec((1, gw), lambda i, j: (0, 0)),
                  pl.BlockSpec((gw, gw), lambda i, j: (0, 0))],
        out_specs=pl.BlockSpec((tm, tn), lambda i, j: (i, j)),
        compiler_params=_params("parallel", "parallel"),
        name="qk_proj_norm",
    )(a, b, g, gm)


def _diff_lambda(lq1, lk1, lq2, lk2, lam_init):
    a = jnp.exp(jnp.sum(lq1[...] * lk1[...], axis=-1, keepdims=True))
    b = jnp.exp(jnp.sum(lq2[...] * lk2[...], axis=-1, keepdims=True))
    return a - b + lam_init


def _finish_heads(a1, l1, a2, l2, lam, sub, lam_init):
    o = a1 / l1 - lam * (a2 / l2)
    y = o * lax.rsqrt(jnp.mean(o * o, axis=-1, keepdims=True) + EPS)
    return (y * sub) * (1.0 - lam_init)


def _split_maps(q, dk):
    lane = lax.broadcasted_iota(jnp.int32, q.shape, 1)
    zero = jnp.zeros_like(q)
    return jnp.where(lane < dk, q, zero), jnp.where(lane >= dk, q, zero)


def _qk(q, k):
    return lax.dot_general(q, k, (((1,), (1,)), ((), ())), preferred_element_type=F32)


def _attn_prompt_kernel(q_ref, k_ref, v_ref, lq1, lk1, lq2, lk2, sub_ref, o_ref,
                        kb, vt, qt, m_s, l_s, acc, s_ref, p_ref, *, tq, dk, lam_init):
    qi = pl.program_id(2)
    hb, nk, tk = kb.shape[0], kb.shape[1], kb.shape[2]
    hw = 2 * dk
    kpq = tq // tk

    @pl.when(qi == 0)
    def _():
        for hh in range(hb):
            for jj in range(nk):
                kv = (slice(jj * tk, (jj + 1) * tk), slice(hh * hw, (hh + 1) * hw))
                kb[hh, jj] = k_ref[kv].astype(BF16)
                vt[hh, jj] = v_ref[kv].T.astype(BF16)

    for hh in range(hb):
        q = q_ref[:, hh * hw:(hh + 1) * hw].astype(F32)
        lane = lax.broadcasted_iota(jnp.int32, q.shape, 1)
        qt[hh, :, 0:tq] = jnp.where(lane < dk, q, 0.0).T.astype(BF16)
        qt[hh, :, tq:2 * tq] = jnp.where(lane >= dk, q, 0.0).T.astype(BF16)
    m_s[...] = jnp.full(m_s.shape, NEG_INF, F32)
    l_s[...] = jnp.zeros(l_s.shape, F32)
    acc[...] = jnp.zeros(acc.shape, F32)

    def block(j, key0):
        for hh in range(hb):
            s_ref[hh] = _dot(kb[hh, j], qt[hh])
        for hh in range(hb):
            for c in range(0, 2 * tq, LANES):
                if key0 is not None and (c % tq) + LANES <= key0:
                    p_ref[hh, :, c:c + LANES] = jnp.zeros((tk, LANES), BF16)
                    continue
                s = s_ref[hh, :, c:c + LANES]
                if key0 is not None:
                    key = lax.broadcasted_iota(jnp.int32, s.shape, 0) + key0
                    qry = lax.broadcasted_iota(jnp.int32, s.shape, 1) + (c % tq)
                    s = jnp.where(key // CHUNK <= qry // CHUNK, s, NEG_INF)
                m_old = m_s[hh, :, c:c + LANES]
                m_new = jnp.maximum(m_old, jnp.max(s, axis=0, keepdims=True))
                alpha = jnp.exp(m_old - m_new)
                p = jnp.exp(s - m_new)
                l_s[hh, :, c:c + LANES] = (alpha * l_s[hh, :, c:c + LANES]
                                           + jnp.sum(p, axis=0, keepdims=True))
                m_s[hh, :, c:c + LANES] = m_new
                p_ref[hh, :, c:c + LANES] = p.astype(BF16)
                acc[hh, :, c:c + LANES] = alpha * acc[hh, :, c:c + LANES]
            acc[hh] += _dot(vt[hh, j], p_ref[hh])

    def body(j, c):
        block(j, None)
        return c

    lax.fori_loop(0, qi * kpq, body, 0)
    for d in range(kpq):
        block(qi * kpq + d, d * tk)
    lam = _diff_lambda(lq1, lk1, lq2, lk2, lam_init)
    for hh in range(hb):
        r = acc[hh] / l_s[hh]
        o = (r[:, 0:tq] - lam * r[:, tq:2 * tq]).T
        y = o * lax.rsqrt(jnp.mean(o * o, axis=-1, keepdims=True) + EPS)
        o_ref[:, hh * hw:(hh + 1) * hw] = ((y * sub_ref[...])
                                           * (1.0 - lam_init)).astype(o_ref.dtype)


def _attn_prompt(q, kv_src, k_col, v_src, v_col, nb, seq, heads, dk, lam_rows, sub, lam_init):
    hw = 2 * dk
    tq = _tile(seq, ATTN_BLOCK, LANES)
    tk = _tile(tq, ATTN_KEY_BLOCK, CHUNK)
    nq, nk = seq // tq, seq // tk
    hb = next(n for n in (ATTN_HEADS, 1) if heads % n == 0)
    bw = hb * hw
    assert (k_col * hw) % bw == 0 and (v_col * hw) % bw == 0
    kc, vc = k_col * hw // bw, v_col * hw // bw
    vec = lambda i, h, j: (0, 0)
    return pl.pallas_call(
        functools.partial(_attn_prompt_kernel, tq=tq, dk=dk, lam_init=lam_init),
        out_shape=jax.ShapeDtypeStruct((nb * seq, heads * hw), BF16),
        grid=(nb, heads // hb, nq),
        in_specs=[pl.BlockSpec((tq, bw), lambda b, h, j: (b * nq + j, h)),
                  pl.BlockSpec((seq, bw), lambda b, h, j: (b, kc + h)),
                  pl.BlockSpec((seq, bw), lambda b, h, j: (b, vc + h)),
                  pl.BlockSpec((1, dk), vec), pl.BlockSpec((1, dk), vec),
                  pl.BlockSpec((1, dk), vec), pl.BlockSpec((1, dk), vec),
                  pl.BlockSpec((1, hw), vec)],
        out_specs=pl.BlockSpec((tq, bw), lambda b, h, j: (b * nq + j, h)),
        scratch_shapes=[pltpu.VMEM((hb, nk, tk, hw), BF16), pltpu.VMEM((hb, nk, hw, tk), BF16),
                        pltpu.VMEM((hb, hw, 2 * tq), BF16), pltpu.VMEM((hb, 1, 2 * tq), F32),
                        pltpu.VMEM((hb, 1, 2 * tq), F32), pltpu.VMEM((hb, hw, 2 * tq), F32),
                        pltpu.VMEM((hb, tk, 2 * tq), F32), pltpu.VMEM((hb, tk, 2 * tq), BF16)],
        compiler_params=_params("parallel", "parallel", "arbitrary"),
        name="attn_prompt",
    )(q, kv_src, v_src, *lam_rows, sub)


def _attn_sample_kernel(q_ref, kn_ref, vn_ref, kp_ref, vp_ref, lq1, lk1, lq2, lk2, sub_ref, o_ref,
                        *, dk, lam_init):
    hw = 2 * dk
    lam = _diff_lambda(lq1, lk1, lq2, lk2, lam_init)
    for c in range(0, q_ref.shape[1], hw):
        q1, q2 = _split_maps(q_ref[:, c:c + hw], dk)
        kp = kp_ref[:, c:c + hw].astype(BF16)
        vp = vp_ref[:, c:c + hw].astype(BF16)
        kn = kn_ref[:, c:c + hw].astype(BF16)
        vn = vn_ref[:, c:c + hw].astype(BF16)

        def one_map(q):
            sp = _qk(q, kp)
            sn = _qk(q, kn)
            m = jnp.maximum(jnp.max(sp, axis=-1, keepdims=True),
                            jnp.max(sn, axis=-1, keepdims=True))
            ep = jnp.exp(sp - m)
            en = jnp.exp(sn - m)
            l = jnp.sum(ep, axis=-1, keepdims=True) + jnp.sum(en, axis=-1, keepdims=True)
            a = _dot(ep.astype(BF16), vp) + _dot(en.astype(BF16), vn)
            return l, a

        l1, a1 = one_map(q1)
        l2, a2 = one_map(q2)
        o_ref[:, c:c + hw] = _finish_heads(a1, l1, a2, l2, lam, sub_ref[...],
                                           lam_init).astype(o_ref.dtype)


def _attn_sample(q, k_new, v_src, v_col, past_k, past_v, layer, row_off, nb, seq, heads, dk,
                 lam_rows, sub, lam_init):
    hw = 2 * dk
    past = past_k.shape[2]
    rb = row_off // seq
    hb = next(n for n in (4, 2, 1) if heads % n == 0)
    bw = hb * hw
    assert (v_col * hw) % bw == 0
    vc = v_col * hw // bw
    vec = lambda b, h: (0, 0)
    return pl.pallas_call(
        functools.partial(_attn_sample_kernel, dk=dk, lam_init=lam_init),
        out_shape=jax.ShapeDtypeStruct((nb * seq, heads * hw), BF16),
        grid=(nb, heads // hb),
        in_specs=[pl.BlockSpec((seq, bw), lambda b, h: (rb + b, h)),
                  pl.BlockSpec((seq, bw), lambda b, h: (rb + b, h)),
                  pl.BlockSpec((seq, bw), lambda b, h: (rb + b, vc + h)),
                  pl.BlockSpec((None, None, past, bw), lambda b, h: (layer, b, 0, h)),
                  pl.BlockSpec((None, None, past, bw), lambda b, h: (layer, b, 0, h)),
                  pl.BlockSpec((1, dk), vec), pl.BlockSpec((1, dk), vec),
                  pl.BlockSpec((1, dk), vec), pl.BlockSpec((1, dk), vec),
                  pl.BlockSpec((1, hw), vec)],
        out_specs=pl.BlockSpec((seq, bw), lambda b, h: (b, h)),
        compiler_params=_params("parallel", "parallel"),
        name="attn_sample",
    )(q, k_new, v_src, past_k, past_v, *lam_rows, sub)


def _conv_kernel(bg_ref, cg_ref, hc_ref, st_ref, cw_ref, y_ref, nc_ref, carry, *, ts):
    s = pl.program_id(2)
    u = cg_ref[...] * hc_ref[...]

    @pl.when(s == 0)
    def _():
        carry[6:8, :] = st_ref[...]

    prev = carry[...]
    p2 = prev[6:7, :]
    p1 = prev[7:8, :]
    row = lax.broadcasted_iota(jnp.int32, u.shape, 0)
    u1 = jnp.where(row == 0, p1, pltpu.roll(u, 1, 0))
    u2 = jnp.where(row == 0, p2, jnp.where(row == 1, p1, pltpu.roll(u, 2, 0)))
    cw = cw_ref[...]
    y = bg_ref[...] * (cw[0:1, :] * u2 + cw[1:2, :] * u1 + cw[2:3, :] * u)
    y_ref[...] = y.astype(y_ref.dtype)
    carry[...] = u[ts - 8:ts, :]

    @pl.when(s == pl.num_programs(2) - 1)
    def _():
        nc_ref[...] = u[ts - 2:ts, :]


def _conv(p, col_b, col_c, col_h, width, row_off, nb, seq, state, cw):
    assert seq >= 8 and state.shape[1] == 2 and cw.shape[0] == 3
    ts = _tile(seq, 512, 8)
    tc = _tile(width, 512, LANES)
    ns = seq // ts
    rb = row_off // ts
    ob, oc, oh = col_b // tc, col_c // tc, col_h // tc
    blk = lambda o: pl.BlockSpec((ts, tc), lambda b, c, s: (rb + b * ns + s, o + c))
    return pl.pallas_call(
        functools.partial(_conv_kernel, ts=ts),
        out_shape=(jax.ShapeDtypeStruct((nb * seq, width), BF16),
                   jax.ShapeDtypeStruct((nb, 2, width), F32)),
        grid=(nb, width // tc, ns),
        in_specs=[blk(ob), blk(oc), blk(oh),
                  pl.BlockSpec((None, 2, tc), lambda b, c, s: (b, 0, c)),
                  pl.BlockSpec((3, tc), lambda b, c, s: (0, c))],
        out_specs=(pl.BlockSpec((ts, tc), lambda b, c, s: (b * ns + s, c)),
                   pl.BlockSpec((None, 2, tc), lambda b, c, s: (b, 0, c))),
        scratch_shapes=[pltpu.VMEM((8, tc), F32)],
        compiler_params=_params("parallel", "parallel", "arbitrary"),
        name="short_conv",
    )(p, p, p, state, cw)


def _outproj_kernel(a1_ref, a2_ref, b1_ref, b2_ref, x_ref, o_ref):
    o_ref[...] = x_ref[...] + (_dot(a1_ref[...], b1_ref[...]) + _dot(a2_ref[...], b2_ref[...]))


def _outproj(a1, a2, w, x):
    m, k1 = a1.shape
    k2 = a2.shape[1]
    assert k1 == k2 and w.shape[0] == k1 + k2
    n = w.shape[1]
    tm = _tile(m, ROW_TILE, 16)
    tn = _tile(n, 512, LANES)
    return pl.pallas_call(
        _outproj_kernel,
        out_shape=jax.ShapeDtypeStruct((m, n), F32),
        grid=(m // tm, n // tn),
        in_specs=[pl.BlockSpec((tm, k1), lambda i, j: (i, 0)),
                  pl.BlockSpec((tm, k2), lambda i, j: (i, 0)),
                  pl.BlockSpec((k1, tn), lambda i, j: (0, j)),
                  pl.BlockSpec((k2, tn), lambda i, j: (1, j)),
                  pl.BlockSpec((tm, tn), lambda i, j: (i, j))],
        out_specs=pl.BlockSpec((tm, tn), lambda i, j: (i, j)),
        compiler_params=_params("parallel", "parallel"),
        name="out_proj",
    )(a1, a2, w, w, x)


def _silu_mul(g, u):
    return (g / (1.0 + jnp.exp(-g))) * u


def _gateup_kernel(a_ref, w1_ref, w3_ref, o_ref):
    a = a_ref[...]
    o_ref[...] = _silu_mul(_dot(a, w1_ref[...]), _dot(a, w3_ref[...])).astype(o_ref.dtype)


def _gateup_cast_kernel(a_ref, w1_ref, w3_ref, c_ref, o_ref, co_ref):
    _gateup_kernel(a_ref, w1_ref, w3_ref, o_ref)
    co_ref[...] = c_ref[...].astype(co_ref.dtype)


def _gateup(a, w1, w3, cast_src=None, cast_layer=0):
    m, k = a.shape
    n = w1.shape[1]
    tm = _tile(m, ROW_TILE, 16)
    tn = _tile(n, 512, LANES)
    nj = n // tn
    steps = (m // tm) * nj
    in_specs = [pl.BlockSpec((tm, k), lambda i, j: (i, 0)),
                pl.BlockSpec((k, tn), lambda i, j: (0, j)),
                pl.BlockSpec((k, tn), lambda i, j: (0, j))]
    out_spec = pl.BlockSpec((tm, tn), lambda i, j: (i, j))
    out_shape = jax.ShapeDtypeStruct((m, n), BF16)
    if cast_src is None:
        return pl.pallas_call(
            _gateup_kernel, out_shape=out_shape, grid=(m // tm, nj), in_specs=in_specs,
            out_specs=out_spec, compiler_params=_params("parallel", "parallel"),
            name="ffn_gate_up",
        )(a, w1, w3), None
    _, cr, cc = cast_src.shape
    assert cr % steps == 0 and (cr // steps) % 16 == 0, "cast rows must split evenly over steps"
    cb = cr // steps
    return pl.pallas_call(
        _gateup_cast_kernel,
        out_shape=(out_shape, jax.ShapeDtypeStruct((cr, cc), BF16)),
        grid=(m // tm, nj),
        in_specs=in_specs + [pl.BlockSpec((None, cb, cc),
                                          lambda i, j: (cast_layer, i * nj + j, 0))],
        out_specs=(out_spec, pl.BlockSpec((cb, cc), lambda i, j: (i * nj + j, 0))),
        compiler_params=_params("parallel", "parallel"),
        name="ffn_gate_up",
    )(a, w1, w3, cast_src)


def _down_kernel(a_ref, b_ref, x_ref, o_ref):
    d = _dot(a_ref[...], b_ref[...])

    @pl.when(pl.program_id(2) == 0)
    def _():
        o_ref[...] = x_ref[...] + d

    @pl.when(pl.program_id(2) > 0)
    def _():
        o_ref[...] += d


def _down(a, w, x):
    m, k = a.shape
    n = w.shape[1]
    tm = _tile(m, ROW_TILE, 16)
    tn = _tile(n, 1024, LANES)
    tk = _tile(k, 3584, MXU_DIM)
    return pl.pallas_call(
        _down_kernel,
        out_shape=jax.ShapeDtypeStruct((m, n), F32),
        grid=(m // tm, n // tn, k // tk),
        in_specs=[pl.BlockSpec((tm, tk), lambda i, j, kk: (i, kk)),
                  pl.BlockSpec((tk, tn), lambda i, j, kk: (kk, j)),
                  pl.BlockSpec((tm, tn), lambda i, j, kk: (i, j))],
        out_specs=pl.BlockSpec((tm, tn), lambda i, j, kk: (i, j)),
        compiler_params=_params("parallel", "parallel", "arbitrary"),
        name="ffn_down",
    )(a, w, x)


def _grouped_gateup_kernel(te_ref, na_ref, rows_ref, a_ref, w1_ref, w3_ref, o_ref, *, sub):
    rows = rows_ref[pl.program_id(0)]
    tm = a_ref.shape[0]

    def compute(lo, size):
        a = a_ref[lo:lo + size, :]
        g = _dot(a, w1_ref[...].astype(BF16))
        u = _dot(a, w3_ref[...].astype(BF16))
        o_ref[lo:lo + size, :] = _silu_mul(g, u).astype(o_ref.dtype)

    @pl.when(rows == tm)
    def _():
        compute(0, tm)

    for s in range(tm // sub):
        @pl.when((rows < tm) & (s * sub < rows))
        def _():
            compute(s * sub, sub)

        @pl.when((rows < tm) & (s * sub >= rows))
        def _():
            o_ref[s * sub:(s + 1) * sub, :] = jnp.zeros((sub, o_ref.shape[1]), o_ref.dtype)


def _grouped_gateup(xg, w1, w3, layer, tile_expert, n_active, tile_rows, tm):
    r, k = xg.shape
    n = w1.shape[3]
    tn = _tile(n, 256, LANES)
    nj = n // tn
    sub = _tile(tm, EXPERT_SUB_ROWS, 16)
    wspec = pl.BlockSpec(
        (None, None, k, tn),
        lambda i, j, te, na, tr: (layer, te[i], 0, jnp.where(i < na[0], j, nj - 1)))
    return pl.pallas_call(
        functools.partial(_grouped_gateup_kernel, sub=sub),
        out_shape=jax.ShapeDtypeStruct((r, n), BF16),
        grid_spec=pltpu.PrefetchScalarGridSpec(
            num_scalar_prefetch=3,
            grid=(r // tm, nj),
            in_specs=[pl.BlockSpec((tm, k),
                                   lambda i, j, te, na, tr: (jnp.minimum(i, na[0] - 1), 0)),
                      wspec, wspec],
            out_specs=pl.BlockSpec((tm, tn), lambda i, j, te, na, tr: (i, j))),
        compiler_params=_params("parallel", "parallel"),
        name="moe_gate_up",
    )(tile_expert, n_active, tile_rows, xg, w1, w3)


def _grouped_down_kernel(te_ref, na_ref, rows_ref, a_ref, b_ref, o_ref, *, sub):
    rows = rows_ref[pl.program_id(0)]
    first = pl.program_id(2) == 0
    tm = a_ref.shape[0]

    def compute(lo, size):
        d = _dot(a_ref[lo:lo + size, :], b_ref[...])

        @pl.when(first)
        def _():
            o_ref[lo:lo + size, :] = d

        @pl.when(jnp.logical_not(first))
        def _():
            o_ref[lo:lo + size, :] += d

    @pl.when(rows == tm)
    def _():
        compute(0, tm)

    for s in range(tm // sub):
        @pl.when((rows < tm) & (s * sub < rows))
        def _():
            compute(s * sub, sub)

        @pl.when((rows < tm) & (s * sub >= rows) & first)
        def _():
            o_ref[s * sub:(s + 1) * sub, :] = jnp.zeros((sub, o_ref.shape[1]), o_ref.dtype)


def _grouped_down(g, w, tile_expert, n_active, tile_rows, tm):
    r, k = g.shape
    n = w.shape[2]
    tn = _tile(n, 1024, LANES)
    tk = _tile(k, 2048, MXU_DIM)
    nj, nk = n // tn, k // tk
    sub = _tile(tm, EXPERT_SUB_ROWS, 16)

    def a_map(i, j, kk, te, na, tr):
        act = i < na[0]
        return jnp.minimum(i, na[0] - 1), jnp.where(act, kk, nk - 1)

    def b_map(i, j, kk, te, na, tr):
        act = i < na[0]
        return te[i], jnp.where(act, kk, nk - 1), jnp.where(act, j, nj - 1)

    return pl.pallas_call(
        functools.partial(_grouped_down_kernel, sub=sub),
        out_shape=jax.ShapeDtypeStruct((r, n), F32),
        grid_spec=pltpu.PrefetchScalarGridSpec(
            num_scalar_prefetch=3,
            grid=(r // tm, nj, nk),
            in_specs=[pl.BlockSpec((tm, tk), a_map),
                      pl.BlockSpec((None, tk, tn), b_map)],
            out_specs=pl.BlockSpec((tm, tn), lambda i, j, kk, te, na, tr: (i, j))),
        compiler_params=_params("parallel", "parallel", "arbitrary"),
        name="moe_down",
    )(tile_expert, n_active, tile_rows, g, w)


def _router_kernel(x_ref, g_ref, wr_ref, h_ref, r_ref, *, n_exp):
    x = x_ref[...]
    h = (x * lax.rsqrt(jnp.mean(x * x, axis=-1, keepdims=True) + EPS)) * g_ref[...]
    h_ref[...] = h
    logits = jnp.dot(h, wr_ref[...], precision=lax.Precision.HIGHEST, preferred_element_type=F32)
    lane = lax.broadcasted_iota(jnp.int32, logits.shape, 1)
    lanef = lane.astype(F32)
    lg = jnp.where(lane < n_exp, logits, -jnp.inf)
    m1 = jnp.max(lg, axis=-1, keepdims=True)
    i1 = jnp.min(jnp.where(lg == m1, lanef, float(LANES)), axis=-1, keepdims=True)
    lg2 = jnp.where(lanef == i1, -jnp.inf, lg)
    m2 = jnp.max(lg2, axis=-1, keepdims=True)
    i2 = jnp.min(jnp.where(lg2 == m2, lanef, float(LANES)), axis=-1, keepdims=True)
    e = jnp.exp(m2 - m1)
    den = 1.0 + e
    r_ref[...] = jnp.where(lane == 0, i1,
                 jnp.where(lane == 1, i2,
                 jnp.where(lane == 2, 1.0 / den,
                 jnp.where(lane == 3, e / den, 0.0))))


def _router(x, g, w_router):
    t, d = x.shape
    n_exp = w_router.shape[1]
    assert n_exp <= LANES and TOP_K == 2
    wr = jnp.zeros((d, LANES), F32).at[:, :n_exp].set(w_router.astype(F32))
    tr = _tile(t, 256, 8)
    return pl.pallas_call(
        functools.partial(_router_kernel, n_exp=n_exp),
        out_shape=(jax.ShapeDtypeStruct((t, d), F32), jax.ShapeDtypeStruct((t, LANES), F32)),
        grid=(t // tr,),
        in_specs=[pl.BlockSpec((tr, d), lambda i: (i, 0)),
                  pl.BlockSpec((1, d), lambda i: (0, 0)),
                  pl.BlockSpec((d, LANES), lambda i: (0, 0))],
        out_specs=(pl.BlockSpec((tr, d), lambda i: (i, 0)),
                   pl.BlockSpec((tr, LANES), lambda i: (i, 0))),
        compiler_params=_params("parallel"),
        name="ffn_norm_router",
    )(x, g.reshape(1, d), wr)


def _row_copy(src_hbm, idx, dst_vmem, row, sem):
    return pltpu.make_async_copy(src_hbm.at[pl.ds(idx, 1), :], dst_vmem.at[pl.ds(row, 1), :], sem)


def _gather_kernel(na_ref, src_ref, nxt_ref, h_hbm, o_ref, buf, sem, *, rows):
    i = pl.program_id(0)
    slot = i % 2

    def issue(idx_ref, sl):
        def body(r, c):
            _row_copy(h_hbm, idx_ref[0, 0, r], buf.at[sl], r, sem.at[sl]).start()
            return c
        lax.fori_loop(0, rows, body, 0, unroll=8)

    @pl.when(i == 0)
    def _():
        issue(src_ref, 0)

    @pl.when(i + 1 < na_ref[0])
    def _():
        issue(nxt_ref, 1 - slot)

    @pl.when(i < na_ref[0])
    def _():
        def wait(r, c):
            _row_copy(h_hbm, src_ref[0, 0, r], buf.at[slot], r, sem.at[slot]).wait()
            return c
        lax.fori_loop(0, rows, wait, 0, unroll=8)
        o_ref[...] = buf[slot].astype(o_ref.dtype)

    @pl.when(i >= na_ref[0])
    def _():
        o_ref[...] = jnp.zeros_like(o_ref)


def _gather_rows(h, src, n_active_steps, rows):
    r = src.shape[0]
    d = h.shape[1]
    steps = r // rows
    src3 = src.reshape(steps, 1, rows)
    idx_spec = lambda f: pl.BlockSpec((1, 1, rows), f, memory_space=pltpu.SMEM)
    return pl.pallas_call(
        functools.partial(_gather_kernel, rows=rows),
        out_shape=jax.ShapeDtypeStruct((r, d), BF16),
        grid_spec=pltpu.PrefetchScalarGridSpec(
            num_scalar_prefetch=1,
            grid=(steps,),
            in_specs=[idx_spec(lambda i, na: (i, 0, 0)),
                      idx_spec(lambda i, na: (jnp.minimum(i + 1, steps - 1), 0, 0)),
                      pl.BlockSpec(memory_space=pl.ANY)],
            out_specs=pl.BlockSpec((rows, d), lambda i, na: (i, 0)),
            scratch_shapes=[pltpu.VMEM((2, rows, d), F32), pltpu.SemaphoreType.DMA((2,))]),
        compiler_params=_params("arbitrary"),
        name="moe_gather",
    )(n_active_steps, src3, src3, h)


def _combine_kernel(dst_ref, eo_hbm, x_ref, r_ref, o_ref, buf, sem, *, rows):
    def start(r, c):
        _row_copy(eo_hbm, dst_ref[0, 0, r], buf.at[0], r, sem.at[0]).start()
        _row_copy(eo_hbm, dst_ref[0, 0, rows + r], buf.at[1], r, sem.at[0]).start()
        return c

    def wait(r, c):
        _row_copy(eo_hbm, dst_ref[0, 0, r], buf.at[0], r, sem.at[0]).wait()
        _row_copy(eo_hbm, dst_ref[0, 0, rows + r], buf.at[1], r, sem.at[0]).wait()
        return c

    lax.fori_loop(0, rows, start, 0, unroll=8)
    lax.fori_loop(0, rows, wait, 0, unroll=8)
    route = r_ref[...]
    o_ref[...] = x_ref[...] + (route[:, 2:3] * buf[0] + route[:, 3:4] * buf[1])


def _combine(eo, dst, x, route, rows):
    t, d = x.shape
    steps = t // rows
    return pl.pallas_call(
        functools.partial(_combine_kernel, rows=rows),
        out_shape=jax.ShapeDtypeStruct((t, d), F32),
        grid=(steps,),
        in_specs=[pl.BlockSpec((1, 1, 2 * rows), lambda i: (i, 0, 0), memory_space=pltpu.SMEM),
                  pl.BlockSpec(memory_space=pl.ANY),
                  pl.BlockSpec((rows, d), lambda i: (i, 0)),
                  pl.BlockSpec((rows, LANES), lambda i: (i, 0))],
        out_specs=pl.BlockSpec((rows, d), lambda i: (i, 0)),
        scratch_shapes=[pltpu.VMEM((2, rows, d), F32), pltpu.SemaphoreType.DMA((1,))],
        compiler_params=_params("arbitrary"),
        name="moe_combine",
    )(dst, eo, x, route)


def _moe_ffn(x, g, w_router, w1, w3, layer, w2):
    t, d = x.shape
    n_exp = w_router.shape[1]
    tm = EXPERT_ROW_TILE
    rows_g = _tile(tm, GATHER_ROWS, 8)
    rows_c = _tile(t, COMBINE_ROWS, 8)
    n_tiles = -(-(TOP_K * t) // tm) + n_exp
    r = n_tiles * tm

    h, route = _router(x, g, w_router)

    e_flat = jnp.concatenate([route[:, 0], route[:, 1]]).astype(jnp.int32)
    onehot = (e_flat[:, None] == jnp.arange(n_exp, dtype=jnp.int32)[None, :]).astype(jnp.int32)
    rank = jnp.sum((jnp.cumsum(onehot, axis=0) - onehot) * onehot, axis=1)
    counts = jnp.sum(onehot, axis=0)
    tiles_per = (counts + tm - 1) // tm
    tile_end = jnp.cumsum(tiles_per)
    tile_start = tile_end - tiles_per
    n_active = tile_end[-1]
    dst = tile_start[e_flat] * tm + rank
    tile_ids = jnp.arange(n_tiles, dtype=jnp.int32)
    tile_expert = jnp.minimum(jnp.searchsorted(tile_end, tile_ids, side="right"), n_exp - 1)
    last_expert = tile_expert[jnp.maximum(n_active - 1, 0)]
    tile_rows = jnp.clip(counts[tile_expert] - (tile_ids - tile_start[tile_expert]) * tm, 0, tm)
    tile_rows = jnp.where(tile_ids < n_active, tile_rows, 0).astype(jnp.int32)
    tile_expert = jnp.where(tile_ids < n_active, tile_expert, last_expert).astype(jnp.int32)
    order = jnp.argsort(e_flat, stable=True).astype(jnp.int32)
    group_off = jnp.cumsum(counts) - counts
    slot = jnp.arange(r, dtype=jnp.int32)
    slot_e = tile_expert[slot // tm]
    within = slot - tile_start[slot_e] * tm
    valid = (within < counts[slot_e]) & (slot // tm < n_active)
    src = jnp.where(valid, order[jnp.clip(group_off[slot_e] + within, 0, TOP_K * t - 1)] % t, 0)
    src = src.astype(jnp.int32)

    na = n_active.astype(jnp.int32).reshape(1)
    xg = _gather_rows(h, src, na * (tm // rows_g), rows_g)
    gu = _grouped_gateup(xg, w1, w3, layer, tile_expert, na, tile_rows, tm)
    eo = _grouped_down(gu, w2, tile_expert, na, tile_rows, tm)
    steps = t // rows_c
    dst2 = jnp.concatenate([dst[:t].reshape(steps, 1, rows_c), dst[t:].reshape(steps, 1, rows_c)],
                           axis=2).astype(jnp.int32)
    return _combine(eo, dst2, x, route, rows_c)


def kernel(x_prompt, x_sample, cache_k, cache_v, cache_conv, w_in, w_out, conv_w, attn_norm,
           q_norm, k_norm, lambda_q1, lambda_k1, lambda_q2, lambda_k2, subln, ffn_norm,
           w1_dense, w3_dense, w2_dense, w_router, w1_exp, w3_exp, w2_exp):
    depth = w_in.shape[0]
    nbp, seq, d = x_prompt.shape
    nbs, sseq, _ = x_sample.shape
    past, heads, _, dk = cache_k.shape[2:]
    dv = cache_v.shape[-1]
    cdim = cache_conv.shape[-1]
    hw = 2 * dk
    assert dv == hw == LANES, "one head (both maps / the value row) must span one lane tile"
    qk_dim = heads * hw
    attn_dim = heads * dv
    tp, ts = nbp * seq, nbs * sseq
    assert tp % sseq == 0 and seq % CHUNK == 0
    c_q, c_k, c_v = 0, qk_dim, 2 * qk_dim
    c_b = c_v + attn_dim

    x = jnp.concatenate([x_prompt.reshape(tp, d), x_sample.reshape(ts, d)], axis=0)
    zero_state = jnp.zeros((nbp, 2, cdim), F32)
    ck = cache_k.reshape(depth, nbs, past, qk_dim)
    cv = cache_v.reshape(depth, nbs, past, attn_dim)

    ks, vs, convs_p, convs_s = [], [], [], []
    for l in range(depth):
        lam0 = _lambda_init(l)
        lam_rows = [a[l].reshape(1, dk) for a in (lambda_q1, lambda_k1, lambda_q2, lambda_k2)]
        sub = subln[l].reshape(1, dv)

        h = _rmsnorm(x, attn_norm[l])
        wi = w_in[l].astype(BF16)
        q = _proj_norm(h, wi, c_q, qk_dim, q_norm[l], dk, dk ** -0.5, BF16)
        k = _proj_norm(h, wi, c_k, qk_dim, k_norm[l], dk, 1.0, F32)
        v = _matmul(h, wi, c_v, attn_dim, F32)
        pc = _matmul(h, wi, c_b, 3 * cdim, F32)
        o_p = _attn_prompt(q, k, 0, v, 0, nbp, seq, heads, dk, lam_rows, sub, lam0)
        o_s = _attn_sample(q, k, v, 0, ck, cv, l, tp, nbs, sseq, heads, dk, lam_rows, sub, lam0)
        y_p, nc_p = _conv(pc, 0, cdim, 2 * cdim, cdim, 0, nbp, seq, zero_state, conv_w[l])
        y_s, nc_s = _conv(pc, 0, cdim, 2 * cdim, cdim, tp, nbs, sseq, cache_conv[l], conv_w[l])
        x = _outproj(jnp.concatenate([o_p, o_s], axis=0), jnp.concatenate([y_p, y_s], axis=0),
                     w_out[l].astype(BF16), x)
        ks.append(k)
        vs.append(v)
        convs_p.append(nc_p)
        convs_s.append(nc_s)

        m = l // 2
        if l % 2 == 0:
            hf = _rmsnorm(x, ffn_norm[l])
            nxt = w2_exp.reshape(w2_exp.shape[0], -1, d) if l + 1 < depth else None
            gu, w2_next = _gateup(hf, w1_dense[m].astype(BF16), w3_dense[m].astype(BF16), nxt, m)
            x = _down(gu, w2_dense[m].astype(BF16), x)
        else:
            w2 = w2_next.reshape(w2_exp.shape[1:])
            x = _moe_ffn(x, ffn_norm[l], w_router[m], w1_exp, w3_exp, m, w2)

    k_all = jnp.stack(ks)
    v_all = jnp.stack(vs)
    return (x[:tp].reshape(nbp, seq, d),
            x[tp:].reshape(nbs, sseq, d),
            k_all[:, :tp].reshape(depth, nbp, seq, heads, 2, dk),
            v_all[:, :tp].reshape(depth, nbp, seq, heads, dv),
            jnp.stack(convs_p),
            k_all[:, tp:].reshape(depth, nbs, sseq, heads, 2, dk),
            v_all[:, tp:].reshape(depth, nbs, sseq, heads, dv),
            jnp.stack(convs_s))
```

```python
import functools
import math

import jax
import jax.numpy as jnp
from jax import lax
from jax.experimental import pallas as pl
from jax.experimental.pallas import tpu as pltpu

CHUNK = 64
TOP_K = 2
EPS = 1e-6
NEG_INF = -1e30
F32 = jnp.float32
BF16 = jnp.bfloat16

LANES = 128
MXU_DIM = 256
VMEM_LIMIT = 60000 * 1024

ROW_TILE = 1100
EXPERT_ROW_TILE = 1536
EXPERT_SUB_ROWS = 384
GATHER_ROWS = 192
COMBINE_ROWS = 256
ATTN_BLOCK = 512
ATTN_KEY_BLOCK = 256
ATTN_HEADS = 4


def _lambda_init(l):
    return 0.8 - 0.6 * math.exp(-0.3 * l)


def _tile(n, pref, mult):
    if n <= pref:
        return n
    best = None
    for t in range(mult, pref + 1, mult):
        if n % t == 0:
            best = t
    assert best is not None, (n, pref, mult)
    return best


def _params(*sem):
    return pltpu.CompilerParams(dimension_semantics=sem, vmem_limit_bytes=VMEM_LIMIT)


def _dot(a, b):
    return jnp.dot(a, b, preferred_element_type=F32)


def _rmsnorm_kernel(x_ref, g_ref, o_ref):
    x = x_ref[...]
    y = x * lax.rsqrt(jnp.mean(x * x, axis=-1, keepdims=True) + EPS)
    o_ref[...] = (y * g_ref[...]).astype(o_ref.dtype)


def _rmsnorm(x, g):
    t, d = x.shape
    tr = _tile(t, 256, 16)
    return pl.pallas_call(
        _rmsnorm_kernel,
        out_shape=jax.ShapeDtypeStruct((t, d), BF16),
        grid=(t // tr,),
        in_specs=[pl.BlockSpec((tr, d), lambda i: (i, 0)),
                  pl.BlockSpec((1, d), lambda i: (0, 0))],
        out_specs=pl.BlockSpec((tr, d), lambda i: (i, 0)),
        compiler_params=_params("parallel"),
        name="rmsnorm",
    )(x, g.reshape(1, d))


def _mm_kernel(a_ref, b_ref, o_ref):
    o_ref[...] = _dot(a_ref[...], b_ref[...]).astype(o_ref.dtype)


def _matmul(a, b, col_off, width, out_dtype, tn_pref=1024):
    m, k = a.shape
    tm = _tile(m, ROW_TILE, 16)
    tn = _tile(math.gcd(width, col_off) if col_off else width, tn_pref, LANES)
    off = col_off // tn
    return pl.pallas_call(
        _mm_kernel,
        out_shape=jax.ShapeDtypeStruct((m, width), out_dtype),
        grid=(m // tm, width // tn),
        in_specs=[pl.BlockSpec((tm, k), lambda i, j: (i, 0)),
                  pl.BlockSpec((k, tn), lambda i, j: (0, j + off))],
        out_specs=pl.BlockSpec((tm, tn), lambda i, j: (i, j)),
        compiler_params=_params("parallel", "parallel"),
        name="in_proj",
    )(a, b)


def _proj_norm_kernel(a_ref, b_ref, g_ref, gm_ref, o_ref, *, group, scale):
    acc = _dot(a_ref[...], b_ref[...])
    gm = gm_ref[...]
    w = gm.shape[0]
    for c in range(0, acc.shape[1], w):
        x = acc[:, c:c + w]
        sq = x * x
        hi = sq.astype(BF16)
        lo = (sq - hi.astype(F32)).astype(BF16)
        ssq = _dot(hi, gm) + _dot(lo, gm)
        y = x * lax.rsqrt(ssq * (1.0 / group) + EPS)
        o_ref[:, c:c + w] = ((y * g_ref[...]) * scale).astype(o_ref.dtype)


def _proj_norm(a, b, col_off, width, gain, group, scale, out_dtype):
    m, k = a.shape
    gw = MXU_DIM
    tm = _tile(m, ROW_TILE, 16)
    tn = _tile(math.gcd(width, col_off) if col_off else width, 512, gw)
    assert gw % group == 0
    ids = jnp.arange(gw) // group
    gm = (ids[:, None] == ids[None, :]).astype(BF16)
    g = jnp.tile(gain.astype(F32), gw // group).reshape(1, gw)
    off = col_off // tn
    return pl.pallas_call(
        functools.partial(_proj_norm_kernel, group=group, scale=scale),
        out_shape=jax.ShapeDtypeStruct((m, width), out_dtype),
        grid=(m // tm, width // tn),
        in_specs=[pl.BlockSpec((tm, k), lambda i, j: (i, 0)),
                  pl.BlockSpec((k, tn), lambda i, j: (0, j + off)),
                  pl.BlockSpec((1, gw), lambda i, j: (0, 0)),
                  pl.BlockSpec((gw, gw), lambda i, j: (0, 0))],
        out_specs=pl.BlockSpec((tm, tn), lambda i, j: (i, j)),
        compiler_params=_params("parallel", "parallel"),
        name="qk_proj_norm",
    )(a, b, g, gm)


def _diff_lambda(lq1, lk1, lq2, lk2, lam_init):
    a = jnp.exp(jnp.sum(lq1[...] * lk1[...], axis=-1, keepdims=True))
    b = jnp.exp(jnp.sum(lq2[...] * lk2[...], axis=-1, keepdims=True))
    return a - b + lam_init


def _finish_heads(a1, l1, a2, l2, lam, sub, lam_init):
    o = a1 / l1 - lam * (a2 / l2)
    y = o * lax.rsqrt(jnp.mean(o * o, axis=-1, keepdims=True) + EPS)
    return (y * sub) * (1.0 - lam_init)


def _split_maps(q, dk):
    lane = lax.broadcasted_iota(jnp.int32, q.shape, 1)
    zero = jnp.zeros_like(q)
    return jnp.where(lane < dk, q, zero), jnp.where(lane >= dk, q, zero)


def _qk(q, k):
    return lax.dot_general(q, k, (((1,), (1,)), ((), ())), preferred_element_type=F32)


def _attn_prompt_kernel(q_ref, k_ref, v_ref, lq1, lk1, lq2, lk2, sub_ref, o_ref,
                        kb, vt, qt, m_s, l_s, acc, s_ref, p_ref, *, tq, dk, lam_init):
    qi = pl.program_id(2)
    hb, nk, tk = kb.shape[0], kb.shape[1], kb.shape[2]
    hw = 2 * dk
    kpq = tq // tk

    @pl.when(qi == 0)
    def _():
        for hh in range(hb):
            for jj in range(nk):
                kv = (slice(jj * tk, (jj + 1) * tk), slice(hh * hw, (hh + 1) * hw))
                kb[hh, jj] = k_ref[kv].astype(BF16)
                vt[hh, jj] = v_ref[kv].T.astype(BF16)

    for hh in range(hb):
        q = q_ref[:, hh * hw:(hh + 1) * hw].astype(F32)
        lane = lax.broadcasted_iota(jnp.int32, q.shape, 1)
        qt[hh, :, 0:tq] = jnp.where(lane < dk, q, 0.0).T.astype(BF16)
        qt[hh, :, tq:2 * tq] = jnp.where(lane >= dk, q, 0.0).T.astype(BF16)
    m_s[...] = jnp.full(m_s.shape, NEG_INF, F32)
    l_s[...] = jnp.zeros(l_s.shape, F32)
    acc[...] = jnp.zeros(acc.shape, F32)

    def block(j, key0):
        for hh in range(hb):
            s_ref[hh] = _dot(kb[hh, j], qt[hh])
        for hh in range(hb):
            for c in range(0, 2 * tq, LANES):
                if key0 is not None and (c % tq) + LANES <= key0:
                    p_ref[hh, :, c:c + LANES] = jnp.zeros((tk, LANES), BF16)
                    continue
                s = s_ref[hh, :, c:c + LANES]
                if key0 is not None:
                    key = lax.broadcasted_iota(jnp.int32, s.shape, 0) + key0
                    qry = lax.broadcasted_iota(jnp.int32, s.shape, 1) + (c % tq)
                    s = jnp.where(key // CHUNK <= qry // CHUNK, s, NEG_INF)
                m_old = m_s[hh, :, c:c + LANES]
                m_new = jnp.maximum(m_old, jnp.max(s, axis=0, keepdims=True))
                alpha = jnp.exp(m_old - m_new)
                p = jnp.exp(s - m_new)
                l_s[hh, :, c:c + LANES] = (alpha * l_s[hh, :, c:c + LANES]
                                           + jnp.sum(p, axis=0, keepdims=True))
                m_s[hh, :, c:c + LANES] = m_new
                p_ref[hh, :, c:c + LANES] = p.astype(BF16)
                acc[hh, :, c:c + LANES] = alpha * acc[hh, :, c:c + LANES]
            acc[hh] += _dot(vt[hh, j], p_ref[hh])

    def body(j, c):
        block(j, None)
        return c

    lax.fori_loop(0, qi * kpq, body, 0)
    for d in range(kpq):
        block(qi * kpq + d, d * tk)
    lam = _diff_lambda(lq1, lk1, lq2, lk2, lam_init)
    for hh in range(hb):
        r = acc[hh] / l_s[hh]
        o = (r[:, 0:tq] - lam * r[:, tq:2 * tq]).T
        y = o * lax.rsqrt(jnp.mean(o * o, axis=-1, keepdims=True) + EPS)
        o_ref[:, hh * hw:(hh + 1) * hw] = ((y * sub_ref[...])
                                           * (1.0 - lam_init)).astype(o_ref.dtype)


def _attn_prompt(q, kv_src, k_col, v_src, v_col, nb, seq, heads, dk, lam_rows, sub, lam_init):
    hw = 2 * dk
    tq = _tile(seq, ATTN_BLOCK, LANES)
    tk = _tile(tq, ATTN_KEY_BLOCK, CHUNK)
    nq, nk = seq // tq, seq // tk
    hb = next(n for n in (ATTN_HEADS, 1) if heads % n == 0)
    bw = hb * hw
    assert (k_col * hw) % bw == 0 and (v_col * hw) % bw == 0
    kc, vc = k_col * hw // bw, v_col * hw // bw
    vec = lambda i, h, j: (0, 0)
    return pl.pallas_call(
        functools.partial(_attn_prompt_kernel, tq=tq, dk=dk, lam_init=lam_init),
        out_shape=jax.ShapeDtypeStruct((nb * seq, heads * hw), BF16),
        grid=(nb, heads // hb, nq),
        in_specs=[pl.BlockSpec((tq, bw), lambda b, h, j: (b * nq + j, h)),
                  pl.BlockSpec((seq, bw), lambda b, h, j: (b, kc + h)),
                  pl.BlockSpec((seq, bw), lambda b, h, j: (b, vc + h)),
                  pl.BlockSpec((1, dk), vec), pl.BlockSpec((1, dk), vec),
                  pl.BlockSpec((1, dk), vec), pl.BlockSpec((1, dk), vec),
                  pl.BlockSpec((1, hw), vec)],
        out_specs=pl.BlockSpec((tq, bw), lambda b, h, j: (b * nq + j, h)),
        scratch_shapes=[pltpu.VMEM((hb, nk, tk, hw), BF16), pltpu.VMEM((hb, nk, hw, tk), BF16),
                        pltpu.VMEM((hb, hw, 2 * tq), BF16), pltpu.VMEM((hb, 1, 2 * tq), F32),
                        pltpu.VMEM((hb, 1, 2 * tq), F32), pltpu.VMEM((hb, hw, 2 * tq), F32),
                        pltpu.VMEM((hb, tk, 2 * tq), F32), pltpu.VMEM((hb, tk, 2 * tq), BF16)],
        compiler_params=_params("parallel", "parallel", "arbitrary"),
        name="attn_prompt",
    )(q, kv_src, v_src, *lam_rows, sub)


def _attn_sample_kernel(q_ref, kn_ref, vn_ref, kp_ref, vp_ref, lq1, lk1, lq2, lk2, sub_ref, o_ref,
                        *, dk, lam_init):
    hw = 2 * dk
    lam = _diff_lambda(lq1, lk1, lq2, lk2, lam_init)
    for c in range(0, q_ref.shape[1], hw):
        q1, q2 = _split_maps(q_ref[:, c:c + hw], dk)
        kp = kp_ref[:, c:c + hw].astype(BF16)
        vp = vp_ref[:, c:c + hw].astype(BF16)
        kn = kn_ref[:, c:c + hw].astype(BF16)
        vn = vn_ref[:, c:c + hw].astype(BF16)

        def one_map(q):
            sp = _qk(q, kp)
            sn = _qk(q, kn)
            m = jnp.maximum(jnp.max(sp, axis=-1, keepdims=True),
                            jnp.max(sn, axis=-1, keepdims=True))
            ep = jnp.exp(sp - m)
            en = jnp.exp(sn - m)
            l = jnp.sum(ep, axis=-1, keepdims=True) + jnp.sum(en, axis=-1, keepdims=True)
            a = _dot(ep.astype(BF16), vp) + _dot(en.astype(BF16), vn)
            return l, a

        l1, a1 = one_map(q1)
        l2, a2 = one_map(q2)
        o_ref[:, c:c + hw] = _finish_heads(a1, l1, a2, l2, lam, sub_ref[...],
                                           lam_init).astype(o_ref.dtype)


def _attn_sample(q, k_new, v_src, v_col, past_k, past_v, layer, row_off, nb, seq, heads, dk,
                 lam_rows, sub, lam_init):
    hw = 2 * dk
    past = past_k.shape[2]
    rb = row_off // seq
    hb = next(n for n in (4, 2, 1) if heads % n == 0)
    bw = hb * hw
    assert (v_col * hw) % bw == 0
    vc = v_col * hw // bw
    vec = lambda b, h: (0, 0)
    return pl.pallas_call(
        functools.partial(_attn_sample_kernel, dk=dk, lam_init=lam_init),
        out_shape=jax.ShapeDtypeStruct((nb * seq, heads * hw), BF16),
        grid=(nb, heads // hb),
        in_specs=[pl.BlockSpec((seq, bw), lambda b, h: (rb + b, h)),
                  pl.BlockSpec((seq, bw), lambda b, h: (rb + b, h)),
                  pl.BlockSpec((seq, bw), lambda b, h: (rb + b, vc + h)),
                  pl.BlockSpec((None, None, past, bw), lambda b, h: (layer, b, 0, h)),
                  pl.BlockSpec((None, None, past, bw), lambda b, h: (layer, b, 0, h)),
                  pl.BlockSpec((1, dk), vec), pl.BlockSpec((1, dk), vec),
                  pl.BlockSpec((1, dk), vec), pl.BlockSpec((1, dk), vec),
                  pl.BlockSpec((1, hw), vec)],
        out_specs=pl.BlockSpec((seq, bw), lambda b, h: (b, h)),
        compiler_params=_params("parallel", "parallel"),
        name="attn_sample",
    )(q, k_new, v_src, past_k, past_v, *lam_rows, sub)


def _conv_kernel(bg_ref, cg_ref, hc_ref, st_ref, cw_ref, y_ref, nc_ref, carry, *, ts):
    s = pl.program_id(2)
    u = cg_ref[...] * hc_ref[...]

    @pl.when(s == 0)
    def _():
        carry[6:8, :] = st_ref[...]

    prev = carry[...]
    p2 = prev[6:7, :]
    p1 = prev[7:8, :]
    row = lax.broadcasted_iota(jnp.int32, u.shape, 0)
    u1 = jnp.where(row == 0, p1, pltpu.roll(u, 1, 0))
    u2 = jnp.where(row == 0, p2, jnp.where(row == 1, p1, pltpu.roll(u, 2, 0)))
    cw = cw_ref[...]
    y = bg_ref[...] * (cw[0:1, :] * u2 + cw[1:2, :] * u1 + cw[2:3, :] * u)
    y_ref[...] = y.astype(y_ref.dtype)
    carry[...] = u[ts - 8:ts, :]

    @pl.when(s == pl.num_programs(2) - 1)
    def _():
        nc_ref[...] = u[ts - 2:ts, :]


def _conv(p, col_b, col_c, col_h, width, row_off, nb, seq, state, cw):
    assert seq >= 8 and state.shape[1] == 2 and cw.shape[0] == 3
    ts = _tile(seq, 512, 8)
    tc = _tile(width, 512, LANES)
    ns = seq // ts
    rb = row_off // ts
    ob, oc, oh = col_b // tc, col_c // tc, col_h // tc
    blk = lambda o: pl.BlockSpec((ts, tc), lambda b, c, s: (rb + b * ns + s, o + c))
    return pl.pallas_call(
        functools.partial(_conv_kernel, ts=ts),
        out_shape=(jax.ShapeDtypeStruct((nb * seq, width), BF16),
                   jax.ShapeDtypeStruct((nb, 2, width), F32)),
        grid=(nb, width // tc, ns),
        in_specs=[blk(ob), blk(oc), blk(oh),
                  pl.BlockSpec((None, 2, tc), lambda b, c, s: (b, 0, c)),
                  pl.BlockSpec((3, tc), lambda b, c, s: (0, c))],
        out_specs=(pl.BlockSpec((ts, tc), lambda b, c, s: (b * ns + s, c)),
                   pl.BlockSpec((None, 2, tc), lambda b, c, s: (b, 0, c))),
        scratch_shapes=[pltpu.VMEM((8, tc), F32)],
        compiler_params=_params("parallel", "parallel", "arbitrary"),
        name="short_conv",
    )(p, p, p, state, cw)


def _outproj_kernel(a1_ref, a2_ref, b1_ref, b2_ref, x_ref, o_ref):
    o_ref[...] = x_ref[...] + (_dot(a1_ref[...], b1_ref[...]) + _dot(a2_ref[...], b2_ref[...]))


def _outproj(a1, a2, w, x):
    m, k1 = a1.shape
    k2 = a2.shape[1]
    assert k1 == k2 and w.shape[0] == k1 + k2
    n = w.shape[1]
    tm = _tile(m, ROW_TILE, 16)
    tn = _tile(n, 512, LANES)
    return pl.pallas_call(
        _outproj_kernel,
        out_shape=jax.ShapeDtypeStruct((m, n), F32),
        grid=(m // tm, n // tn),
        in_specs=[pl.BlockSpec((tm, k1), lambda i, j: (i, 0)),
                  pl.BlockSpec((tm, k2), lambda i, j: (i, 0)),
                  pl.BlockSpec((k1, tn), lambda i, j: (0, j)),
                  pl.BlockSpec((k2, tn), lambda i, j: (1, j)),
                  pl.BlockSpec((tm, tn), lambda i, j: (i, j))],
        out_specs=pl.BlockSpec((tm, tn), lambda i, j: (i, j)),
        compiler_params=_params("parallel", "parallel"),
        name="out_proj",
    )(a1, a2, w, w, x)


def _silu_mul(g, u):
    return (g / (1.0 + jnp.exp(-g))) * u


def _gateup_kernel(a_ref, w1_ref, w3_ref, o_ref):
    a = a_ref[...]
    o_ref[...] = _silu_mul(_dot(a, w1_ref[...]), _dot(a, w3_ref[...])).astype(o_ref.dtype)


def _gateup_cast_kernel(a_ref, w1_ref, w3_ref, c_ref, o_ref, co_ref):
    _gateup_kernel(a_ref, w1_ref, w3_ref, o_ref)
    co_ref[...] = c_ref[...].astype(co_ref.dtype)


def _gateup(a, w1, w3, cast_src=None, cast_layer=0):
    m, k = a.shape
    n = w1.shape[1]
    tm = _tile(m, ROW_TILE, 16)
    tn = _tile(n, 512, LANES)
    nj = n // tn
    steps = (m // tm) * nj
    in_specs = [pl.BlockSpec((tm, k), lambda i, j: (i, 0)),
                pl.BlockSpec((k, tn), lambda i, j: (0, j)),
                pl.BlockSpec((k, tn), lambda i, j: (0, j))]
    out_spec = pl.BlockSpec((tm, tn), lambda i, j: (i, j))
    out_shape = jax.ShapeDtypeStruct((m, n), BF16)
    if cast_src is None:
        return pl.pallas_call(
            _gateup_kernel, out_shape=out_shape, grid=(m // tm, nj), in_specs=in_specs,
            out_specs=out_spec, compiler_params=_params("parallel", "parallel"),
            name="ffn_gate_up",
        )(a, w1, w3), None
    _, cr, cc = cast_src.shape
    assert cr % steps == 0 and (cr // steps) % 16 == 0, "cast rows must split evenly over steps"
    cb = cr // steps
    return pl.pallas_call(
        _gateup_cast_kernel,
        out_shape=(out_shape, jax.ShapeDtypeStruct((cr, cc), BF16)),
        grid=(m // tm, nj),
        in_specs=in_specs + [pl.BlockSpec((None, cb, cc),
                                          lambda i, j: (cast_layer, i * nj + j, 0))],
        out_specs=(out_spec, pl.BlockSpec((cb, cc), lambda i, j: (i * nj + j, 0))),
        compiler_params=_params("parallel", "parallel"),
        name="ffn_gate_up",
    )(a, w1, w3, cast_src)


def _down_kernel(a_ref, b_ref, x_ref, o_ref):
    d = _dot(a_ref[...], b_ref[...])

    @pl.when(pl.program_id(2) == 0)
    def _():
        o_ref[...] = x_ref[...] + d

    @pl.when(pl.program_id(2) > 0)
    def _():
        o_ref[...] += d


def _down(a, w, x):
    m, k = a.shape
    n = w.shape[1]
    tm = _tile(m, ROW_TILE, 16)
    tn = _tile(n, 1024, LANES)
    tk = _tile(k, 3584, MXU_DIM)
    return pl.pallas_call(
        _down_kernel,
        out_shape=jax.ShapeDtypeStruct((m, n), F32),
        grid=(m // tm, n // tn, k // tk),
        in_specs=[pl.BlockSpec((tm, tk), lambda i, j, kk: (i, kk)),
                  pl.BlockSpec((tk, tn), lambda i, j, kk: (kk, j)),
                  pl.BlockSpec((tm, tn), lambda i, j, kk: (i, j))],
        out_specs=pl.BlockSpec((tm, tn), lambda i, j, kk: (i, j)),
        compiler_params=_params("parallel", "parallel", "arbitrary"),
        name="ffn_down",
    )(a, w, x)


def _grouped_gateup_kernel(te_ref, na_ref, rows_ref, a_ref, w1_ref, w3_ref, o_ref, *, sub):
    rows = rows_ref[pl.program_id(0)]
    tm = a_ref.shape[0]

    def compute(lo, size):
        a = a_ref[lo:lo + size, :]
        g = _dot(a, w1_ref[...].astype(BF16))
        u = _dot(a, w3_ref[...].astype(BF16))
        o_ref[lo:lo + size, :] = _silu_mul(g, u).astype(o_ref.dtype)

    @pl.when(rows == tm)
    def _():
        compute(0, tm)

    for s in range(tm // sub):
        @pl.when((rows < tm) & (s * sub < rows))
        def _():
            compute(s * sub, sub)

        @pl.when((rows < tm) & (s * sub >= rows))
        def _():
            o_ref[s * sub:(s + 1) * sub, :] = jnp.zeros((sub, o_ref.shape[1]), o_ref.dtype)


def _grouped_gateup(xg, w1, w3, layer, tile_expert, n_active, tile_rows, tm):
    r, k = xg.shape
    n = w1.shape[3]
    tn = _tile(n, 256, LANES)
    nj = n // tn
    sub = _tile(tm, EXPERT_SUB_ROWS, 16)
    wspec = pl.BlockSpec(
        (None, None, k, tn),
        lambda i, j, te, na, tr: (layer, te[i], 0, jnp.where(i < na[0], j, nj - 1)))
    return pl.pallas_call(
        functools.partial(_grouped_gateup_kernel, sub=sub),
        out_shape=jax.ShapeDtypeStruct((r, n), BF16),
        grid_spec=pltpu.PrefetchScalarGridSpec(
            num_scalar_prefetch=3,
            grid=(r // tm, nj),
            in_specs=[pl.BlockSpec((tm, k),
                                   lambda i, j, te, na, tr: (jnp.minimum(i, na[0] - 1), 0)),
                      wspec, wspec],
            out_specs=pl.BlockSpec((tm, tn), lambda i, j, te, na, tr: (i, j))),
        compiler_params=_params("parallel", "parallel"),
        name="moe_gate_up",
    )(tile_expert, n_active, tile_rows, xg, w1, w3)


def _grouped_down_kernel(te_ref, na_ref, rows_ref, a_ref, b_ref, o_ref, *, sub):
    rows = rows_ref[pl.program_id(0)]
    first = pl.program_id(2) == 0
    tm = a_ref.shape[0]

    def compute(lo, size):
        d = _dot(a_ref[lo:lo + size, :], b_ref[...])

        @pl.when(first)
        def _():
            o_ref[lo:lo + size, :] = d

        @pl.when(jnp.logical_not(first))
        def _():
            o_ref[lo:lo + size, :] += d

    @pl.when(rows == tm)
    def _():
        compute(0, tm)

    for s in range(tm // sub):
        @pl.when((rows < tm) & (s * sub < rows))
        def _():
            compute(s * sub, sub)

        @pl.when((rows < tm) & (s * sub >= rows) & first)
        def _():
            o_ref[s * sub:(s + 1) * sub, :] = jnp.zeros((sub, o_ref.shape[1]), o_ref.dtype)


def _grouped_down(g, w, tile_expert, n_active, tile_rows, tm):
    r, k = g.shape
    n = w.shape[2]
    tn = _tile(n, 1024, LANES)
    tk = _tile(k, 3584, MXU_DIM)
    nj, nk = n // tn, k // tk
    sub = _tile(tm, EXPERT_SUB_ROWS, 16)

    def a_map(i, j, kk, te, na, tr):
        act = i < na[0]
        return jnp.minimum(i, na[0] - 1), jnp.where(act, kk, nk - 1)

    def b_map(i, j, kk, te, na, tr):
        act = i < na[0]
        return te[i], jnp.where(act, kk, nk - 1), jnp.where(act, j, nj - 1)

    return pl.pallas_call(
        functools.partial(_grouped_down_kernel, sub=sub),
        out_shape=jax.ShapeDtypeStruct((r, n), F32),
        grid_spec=pltpu.PrefetchScalarGridSpec(
            num_scalar_prefetch=3,
            grid=(r // tm, nj, nk),
            in_specs=[pl.BlockSpec((tm, tk), a_map),
                      pl.BlockSpec((None, tk, tn), b_map)],
            out_specs=pl.BlockSpec((tm, tn), lambda i, j, kk, te, na, tr: (i, j))),
        compiler_params=_params("parallel", "parallel", "arbitrary"),
        name="moe_down",
    )(tile_expert, n_active, tile_rows, g, w)


def _router_kernel(x_ref, g_ref, wr_ref, h_ref, r_ref, *, n_exp):
    x = x_ref[...]
    h = (x * lax.rsqrt(jnp.mean(x * x, axis=-1, keepdims=True) + EPS)) * g_ref[...]
    h_ref[...] = h
    logits = jnp.dot(h, wr_ref[...], precision=lax.Precision.HIGHEST, preferred_element_type=F32)
    lane = lax.broadcasted_iota(jnp.int32, logits.shape, 1)
    lanef = lane.astype(F32)
    lg = jnp.where(lane < n_exp, logits, -jnp.inf)
    m1 = jnp.max(lg, axis=-1, keepdims=True)
    i1 = jnp.min(jnp.where(lg == m1, lanef, float(LANES)), axis=-1, keepdims=True)
    lg2 = jnp.where(lanef == i1, -jnp.inf, lg)
    m2 = jnp.max(lg2, axis=-1, keepdims=True)
    i2 = jnp.min(jnp.where(lg2 == m2, lanef, float(LANES)), axis=-1, keepdims=True)
    e = jnp.exp(m2 - m1)
    den = 1.0 + e
    r_ref[...] = jnp.where(lane == 0, i1,
                 jnp.where(lane == 1, i2,
                 jnp.where(lane == 2, 1.0 / den,
                 jnp.where(lane == 3, e / den, 0.0))))


def _router(x, g, w_router):
    t, d = x.shape
    n_exp = w_router.shape[1]
    assert n_exp <= LANES and TOP_K == 2
    wr = jnp.zeros((d, LANES), F32).at[:, :n_exp].set(w_router.astype(F32))
    tr = _tile(t, 256, 8)
    return pl.pallas_call(
        functools.partial(_router_kernel, n_exp=n_exp),
        out_shape=(jax.ShapeDtypeStruct((t, d), F32), jax.ShapeDtypeStruct((t, LANES), F32)),
        grid=(t // tr,),
        in_specs=[pl.BlockSpec((tr, d), lambda i: (i, 0)),
                  pl.BlockSpec((1, d), lambda i: (0, 0)),
                  pl.BlockSpec((d, LANES), lambda i: (0, 0))],
        out_specs=(pl.BlockSpec((tr, d), lambda i: (i, 0)),
                   pl.BlockSpec((tr, LANES), lambda i: (i, 0))),
        compiler_params=_params("parallel"),
        name="ffn_norm_router",
    )(x, g.reshape(1, d), wr)


def _row_copy(src_hbm, idx, dst_vmem, row, sem):
    return pltpu.make_async_copy(src_hbm.at[pl.ds(idx, 1), :], dst_vmem.at[pl.ds(row, 1), :], sem)


def _gather_kernel(na_ref, src_ref, nxt_ref, h_hbm, o_ref, buf, sem, *, rows):
    i = pl.program_id(0)
    slot = i % 2

    def issue(idx_ref, sl):
        def body(r, c):
            _row_copy(h_hbm, idx_ref[0, 0, r], buf.at[sl], r, sem.at[sl]).start()
            return c
        lax.fori_loop(0, rows, body, 0, unroll=8)

    @pl.when(i == 0)
    def _():
        issue(src_ref, 0)

    @pl.when(i + 1 < na_ref[0])
    def _():
        issue(nxt_ref, 1 - slot)

    @pl.when(i < na_ref[0])
    def _():
        def wait(r, c):
            _row_copy(h_hbm, src_ref[0, 0, r], buf.at[slot], r, sem.at[slot]).wait()
            return c
        lax.fori_loop(0, rows, wait, 0, unroll=8)
        o_ref[...] = buf[slot].astype(o_ref.dtype)

    @pl.when(i >= na_ref[0])
    def _():
        o_ref[...] = jnp.zeros_like(o_ref)


def _gather_rows(h, src, n_active_steps, rows):
    r = src.shape[0]
    d = h.shape[1]
    steps = r // rows
    src3 = src.reshape(steps, 1, rows)
    idx_spec = lambda f: pl.BlockSpec((1, 1, rows), f, memory_space=pltpu.SMEM)
    return pl.pallas_call(
        functools.partial(_gather_kernel, rows=rows),
        out_shape=jax.ShapeDtypeStruct((r, d), BF16),
        grid_spec=pltpu.PrefetchScalarGridSpec(
            num_scalar_prefetch=1,
            grid=(steps,),
            in_specs=[idx_spec(lambda i, na: (i, 0, 0)),
                      idx_spec(lambda i, na: (jnp.minimum(i + 1, steps - 1), 0, 0)),
                      pl.BlockSpec(memory_space=pl.ANY)],
            out_specs=pl.BlockSpec((rows, d), lambda i, na: (i, 0)),
            scratch_shapes=[pltpu.VMEM((2, rows, d), F32), pltpu.SemaphoreType.DMA((2,))]),
        compiler_params=_params("arbitrary"),
        name="moe_gather",
    )(n_active_steps, src3, src3, h)


def _combine_kernel(dst_ref, eo_hbm, x_ref, r_ref, o_ref, buf, sem, *, rows):
    def start(r, c):
        _row_copy(eo_hbm, dst_ref[0, 0, r], buf.at[0], r, sem.at[0]).start()
        _row_copy(eo_hbm, dst_ref[0, 0, rows + r], buf.at[1], r, sem.at[0]).start()
        return c

    def wait(r, c):
        _row_copy(eo_hbm, dst_ref[0, 0, r], buf.at[0], r, sem.at[0]).wait()
        _row_copy(eo_hbm, dst_ref[0, 0, rows + r], buf.at[1], r, sem.at[0]).wait()
        return c

    lax.fori_loop(0, rows, start, 0, unroll=8)
    lax.fori_loop(0, rows, wait, 0, unroll=8)
    route = r_ref[...]
    o_ref[...] = x_ref[...] + (route[:, 2:3] * buf[0] + route[:, 3:4] * buf[1])


def _combine(eo, dst, x, route, rows):
    t, d = x.shape
    steps = t // rows
    return pl.pallas_call(
        functools.partial(_combine_kernel, rows=rows),
        out_shape=jax.ShapeDtypeStruct((t, d), F32),
        grid=(steps,),
        in_specs=[pl.BlockSpec((1, 1, 2 * rows), lambda i: (i, 0, 0), memory_space=pltpu.SMEM),
                  pl.BlockSpec(memory_space=pl.ANY),
                  pl.BlockSpec((rows, d), lambda i: (i, 0)),
                  pl.BlockSpec((rows, LANES), lambda i: (i, 0))],
        out_specs=pl.BlockSpec((rows, d), lambda i: (i, 0)),
        scratch_shapes=[pltpu.VMEM((2, rows, d), F32), pltpu.SemaphoreType.DMA((1,))],
        compiler_params=_params("arbitrary"),
        name="moe_combine",
    )(dst, eo, x, route)


def _moe_ffn(x, g, w_router, w1, w3, layer, w2):
    t, d = x.shape
    n_exp = w_router.shape[1]
    tm = EXPERT_ROW_TILE
    rows_g = _tile(tm, GATHER_ROWS, 8)
    rows_c = _tile(t, COMBINE_ROWS, 8)
    n_tiles = -(-(TOP_K * t) // tm) + n_exp
    r = n_tiles * tm

    h, route = _router(x, g, w_router)

    e_flat = jnp.concatenate([route[:, 0], route[:, 1]]).astype(jnp.int32)
    onehot = (e_flat[:, None] == jnp.arange(n_exp, dtype=jnp.int32)[None, :]).astype(jnp.int32)
    rank = jnp.sum((jnp.cumsum(onehot, axis=0) - onehot) * onehot, axis=1)
    counts = jnp.sum(onehot, axis=0)
    tiles_per = (counts + tm - 1) // tm
    tile_end = jnp.cumsum(tiles_per)
    tile_start = tile_end - tiles_per
    n_active = tile_end[-1]
    dst = tile_start[e_flat] * tm + rank
    tile_ids = jnp.arange(n_tiles, dtype=jnp.int32)
    tile_expert = jnp.minimum(jnp.searchsorted(tile_end, tile_ids, side="right"), n_exp - 1)
    last_expert = tile_expert[jnp.maximum(n_active - 1, 0)]
    tile_rows = jnp.clip(counts[tile_expert] - (tile_ids - tile_start[tile_expert]) * tm, 0, tm)
    tile_rows = jnp.where(tile_ids < n_active, tile_rows, 0).astype(jnp.int32)
    tile_expert = jnp.where(tile_ids < n_active, tile_expert, last_expert).astype(jnp.int32)
    order = jnp.argsort(e_flat, stable=True).astype(jnp.int32)
    group_off = jnp.cumsum(counts) - counts
    slot = jnp.arange(r, dtype=jnp.int32)
    slot_e = tile_expert[slot // tm]
    within = slot - tile_start[slot_e] * tm
    valid = (within < counts[slot_e]) & (slot // tm < n_active)
    src = jnp.where(valid, order[jnp.clip(group_off[slot_e] + within, 0, TOP_K * t - 1)] % t, 0)
    src = src.astype(jnp.int32)

    na = n_active.astype(jnp.int32).reshape(1)
    xg = _gather_rows(h, src, na * (tm // rows_g), rows_g)
    gu = _grouped_gateup(xg, w1, w3, layer, tile_expert, na, tile_rows, tm)
    eo = _grouped_down(gu, w2, tile_expert, na, tile_rows, tm)
    steps = t // rows_c
    dst2 = jnp.concatenate([dst[:t].reshape(steps, 1, rows_c), dst[t:].reshape(steps, 1, rows_c)],
                           axis=2).astype(jnp.int32)
    return _combine(eo, dst2, x, route, rows_c)


def kernel(x_prompt, x_sample, cache_k, cache_v, cache_conv, w_in, w_out, conv_w, attn_norm,
           q_norm, k_norm, lambda_q1, lambda_k1, lambda_q2, lambda_k2, subln, ffn_norm,
           w1_dense, w3_dense, w2_dense, w_router, w1_exp, w3_exp, w2_exp):
    depth = w_in.shape[0]
    nbp, seq, d = x_prompt.shape
    nbs, sseq, _ = x_sample.shape
    past, heads, _, dk = cache_k.shape[2:]
    dv = cache_v.shape[-1]
    cdim = cache_conv.shape[-1]
    hw = 2 * dk
    assert dv == hw == LANES, "one head (both maps / the value row) must span one lane tile"
    qk_dim = heads * hw
    attn_dim = heads * dv
    tp, ts = nbp * seq, nbs * sseq
    assert tp % sseq == 0 and seq % CHUNK == 0
    c_q, c_k, c_v = 0, qk_dim, 2 * qk_dim
    c_b = c_v + attn_dim

    x = jnp.concatenate([x_prompt.reshape(tp, d), x_sample.reshape(ts, d)], axis=0)
    zero_state = jnp.zeros((nbp, 2, cdim), F32)
    ck = cache_k.reshape(depth, nbs, past, qk_dim)
    cv = cache_v.reshape(depth, nbs, past, attn_dim)

    ks, vs, convs_p, convs_s = [], [], [], []
    for l in range(depth):
        lam0 = _lambda_init(l)
        lam_rows = [a[l].reshape(1, dk) for a in (lambda_q1, lambda_k1, lambda_q2, lambda_k2)]
        sub = subln[l].reshape(1, dv)

        h = _rmsnorm(x, attn_norm[l])
        wi = w_in[l].astype(BF16)
        q = _proj_norm(h, wi, c_q, qk_dim, q_norm[l], dk, dk ** -0.5, BF16)
        k = _proj_norm(h, wi, c_k, qk_dim, k_norm[l], dk, 1.0, F32)
        v = _matmul(h, wi, c_v, attn_dim, F32)
        pc = _matmul(h, wi, c_b, 3 * cdim, F32)
        o_p = _attn_prompt(q, k, 0, v, 0, nbp, seq, heads, dk, lam_rows, sub, lam0)
        o_s = _attn_sample(q, k, v, 0, ck, cv, l, tp, nbs, sseq, heads, dk, lam_rows, sub, lam0)
        y_p, nc_p = _conv(pc, 0, cdim, 2 * cdim, cdim, 0, nbp, seq, zero_state, conv_w[l])
        y_s, nc_s = _conv(pc, 0, cdim, 2 * cdim, cdim, tp, nbs, sseq, cache_conv[l], conv_w[l])
        x = _outproj(jnp.concatenate([o_p, o_s], axis=0), jnp.concatenate([y_p, y_s], axis=0),
                     w_out[l].astype(BF16), x)
        ks.append(k)
        vs.append(v)
        convs_p.append(nc_p)
        convs_s.append(nc_s)

        m = l // 2
        if l % 2 == 0:
            hf = _rmsnorm(x, ffn_norm[l])
            nxt = w2_exp.reshape(w2_exp.shape[0], -1, d) if l + 1 < depth else None
            gu, w2_next = _gateup(hf, w1_dense[m].astype(BF16), w3_dense[m].astype(BF16), nxt, m)
            x = _down(gu, w2_dense[m].astype(BF16), x)
        else:
            w2 = w2_next.reshape(w2_exp.shape[1:])
            x = _moe_ffn(x, ffn_norm[l], w_router[m], w1_exp, w3_exp, m, w2)

    k_all = jnp.stack(ks)
    v_all = jnp.stack(vs)
    return (x[:tp].reshape(nbp, seq, d),
            x[tp:].reshape(nbs, sseq, d),
            k_all[:, :tp].reshape(depth, nbp, seq, heads, 2, dk),
            v_all[:, :tp].reshape(depth, nbp, seq, heads, dv),
            jnp.stack(convs_p),
            k_all[:, tp:].reshape(depth, nbs, sseq, heads, 2, dk),
            v_all[:, tp:].reshape(depth, nbs, sseq, heads, dv),
            jnp.stack(convs_s))
```
